```python
import jax, jax.numpy as jnp
from jax import lax
import numpy as np

D_MODEL = 1024
BATCH = 8
SEQ = 2048
DEPTH = 4
DEC_BATCH = 128
DEC_SEQ = 4
PAST_LEN = 16384
PAGE_SIZE = 128

HEAD_SIZE = 64
N_HEADS = D_MODEL // HEAD_SIZE
D_RWKV = N_HEADS * HEAD_SIZE
D_CONV = D_MODEL // 2
CONV_WIDTH = 31
LORA_DECAY = 64
LORA_ICLR = 64
LORA_GATE = 128
D_FF = 2816
SHIFT_W = 3 * D_RWKV + LORA_DECAY + LORA_ICLR + LORA_GATE
IN_W = SHIFT_W + 2 * D_CONV + 2 * D_MODEL
NORM_EPS = 1e-6
GN_EPS = 64e-5
LN_EPS = 1e-5

kernel_name = "rwkv7_conformer_conv_gated_hybrid_step"


def _rmsnorm(x, g):
    xf = x.astype(jnp.float32)
    y = xf * lax.rsqrt(jnp.mean(xf * xf, axis=-1, keepdims=True) + NORM_EPS)
    return (y * g.astype(jnp.float32)).astype(x.dtype)


def _swiglu(h, wg, wu, wd):
    return (jax.nn.silu(h @ wg) * (h @ wu)) @ wd


def _wkv_scan(S0, r, w, k, v, a_vec, b_vec):
    def step(S, inp):
        r_t, w_t, k_t, v_t, a_t, b_t = inp
        sa = jnp.einsum('bhij,bhj->bhi', S, a_t)
        S = S * w_t[:, :, None, :] + sa[..., None] * b_t[:, :, None, :] + v_t[..., None] * k_t[:, :, None, :]
        y = jnp.einsum('bhij,bhj->bhi', S, r_t)
        return S, y
    seq = tuple(jnp.swapaxes(t, 0, 1) for t in (r, w, k, v, a_vec, b_vec))
    S, ys = lax.scan(step, S0, seq)
    return jnp.swapaxes(ys, 0, 1), S


def _mixer(h, shift_prev, S0, conv_buf, p):
    B, T, _ = h.shape
    y = h @ p['w_in']
    ys = y[..., :SHIFT_W]
    ys_prev_row = shift_prev @ p['w_in'][:, :SHIFT_W]
    ys_prev = jnp.concatenate([ys_prev_row[:, None, :], ys[:, :-1]], axis=1)
    xs = (ys + (ys_prev - ys) * p['mu']).astype(jnp.float32)
    o0 = D_RWKV; o1 = 2 * D_RWKV; o2 = 3 * D_RWKV
    o3 = o2 + LORA_DECAY; o4 = o3 + LORA_ICLR
    r, k, v = xs[..., :o0], xs[..., o0:o1], xs[..., o1:o2]
    xw, xa, xg = xs[..., o2:o3], xs[..., o3:o4], xs[..., o4:]
    f32 = lambda t: t.astype(jnp.float32)
    w_log = -jax.nn.softplus(-(f32(p['w0']) + jnp.tanh(xw) @ f32(p['w_decay_up']))) - 0.5
    decay = jnp.exp(-jnp.exp(w_log))
    a = jax.nn.sigmoid(f32(p['a0']) + xa @ f32(p['w_iclr_up']))
    g = jax.nn.sigmoid(xg) @ f32(p['w_gate_up'])
    hs = lambda t: t.reshape(B, T, N_HEADS, HEAD_SIZE)
    kk = hs(k * f32(p['k_k']))
    kk = kk / jnp.maximum(jnp.sqrt(jnp.sum(kk * kk, axis=-1, keepdims=True)), 1e-12)
    k = k * (1.0 + (a - 1.0) * f32(p['k_a']))
    rh, kh, vh, ah = hs(r), hs(k), hs(v), hs(a)
    out, S_new = _wkv_scan(S0.astype(jnp.float32), rh, hs(decay), kh, vh, -kk, kk * ah)
    mean = jnp.mean(out, axis=-1, keepdims=True)
    var = jnp.mean(jnp.square(out - mean), axis=-1, keepdims=True)
    gn = ((out - mean) * lax.rsqrt(var + GN_EPS)).reshape(B, T, D_RWKV)
    gn = gn * f32(p['ln_x_g']) + f32(p['ln_x_b'])
    bonus = jnp.sum(rh * kh * f32(p['r_k']), axis=-1, keepdims=True) * vh
    o_a = ((gn + bonus.reshape(B, T, D_RWKV)) * g).astype(h.dtype)
    c = y[..., SHIFT_W:SHIFT_W + 2 * D_CONV]
    u = c[..., :D_CONV] * jax.nn.sigmoid(c[..., D_CONV:])
    up = jnp.concatenate([conv_buf.astype(u.dtype), u], axis=1)
    dw = lax.conv_general_dilated(up, p['conv_w'][:, None, :], (1,), 'VALID',
                                  dimension_numbers=('NWC', 'WIO', 'NWC'),
                                  feature_group_count=D_CONV) + p['conv_b']
    new_buf = up[:, -(CONV_WIDTH - 1):]
    dwf = dw.astype(jnp.float32)
    m = jnp.mean(dwf, axis=-1, keepdims=True)
    vv = jnp.mean(jnp.square(dwf - m), axis=-1, keepdims=True)
    ln = ((dwf - m) * lax.rsqrt(vv + LN_EPS) * f32(p['conv_ln_g']) + f32(p['conv_ln_b'])).astype(h.dtype)
    o_b = jax.nn.silu(ln) @ p['w_conv_out']
    g_a = jax.nn.sigmoid(y[..., SHIFT_W + 2 * D_CONV:SHIFT_W + 2 * D_CONV + D_MODEL])
    g_b = jax.nn.sigmoid(y[..., SHIFT_W + 2 * D_CONV + D_MODEL:])
    z = g_a * o_a + g_b * o_b
    return z @ p['w_out'], h[:, -1], S_new, new_buf


def _layer(x, shift_prev, S0, conv_buf, p):
    x = x + 0.5 * _swiglu(_rmsnorm(x, p['ffn1_norm']), p['ffn1_w_gate'], p['ffn1_w_up'], p['ffn1_w_down'])
    h = _rmsnorm(x, p['mix_norm'])
    mix, new_shift, S_new, new_buf = _mixer(h, shift_prev, S0, conv_buf, p)
    x = x + mix
    x = x + 0.5 * _swiglu(_rmsnorm(x, p['ffn2_norm']), p['ffn2_w_gate'], p['ffn2_w_up'], p['ffn2_w_down'])
    return x, new_shift, S_new, new_buf


def setup_inputs(seed: int = 0) -> dict:
    key = jax.random.key(seed)
    ks = iter(jax.random.split(key, 40))
    nrm = lambda shape, s: jax.random.normal(next(ks), shape, jnp.float32) * s
    gain = lambda shape: 1.0 + nrm(shape, 0.02)
    L, D, F, DA, DC = DEPTH, D_MODEL, D_FF, D_RWKV, D_CONV
    w0_base = jnp.linspace(-6.0, -1.0, DA, dtype=jnp.float32)
    return {
        'x_prompt': nrm((BATCH, SEQ, D), 1.0),
        'x_sample': nrm((DEC_BATCH, DEC_SEQ, D), 1.0),
        'state_wkv': nrm((L, DEC_BATCH, N_HEADS, HEAD_SIZE, HEAD_SIZE), 0.3),
        'state_shift': nrm((L, DEC_BATCH, D), 1.0),
        'state_conv': nrm((L, DEC_BATCH, CONV_WIDTH - 1, DC), 0.5),
        'ffn1_norm': gain((L, D)),
        'ffn1_w_gate': nrm((L, D, F), D ** -0.5),
        'ffn1_w_up': nrm((L, D, F), D ** -0.5),
        'ffn1_w_down': nrm((L, F, D), F ** -0.5),
        'mix_norm': gain((L, D)),
        'w_in': nrm((L, D, IN_W), D ** -0.5),
        'mu': jax.random.uniform(next(ks), (L, SHIFT_W), jnp.float32),
        'w0': w0_base[None, :] + nrm((L, DA), 0.1),
        'w_decay_up': nrm((L, LORA_DECAY, DA), 0.5 * LORA_DECAY ** -0.5),
        'a0': nrm((L, DA), 0.1),
        'w_iclr_up': nrm((L, LORA_ICLR, DA), 0.5 * LORA_ICLR ** -0.5),
        'w_gate_up': nrm((L, LORA_GATE, DA), LORA_GATE ** -0.5),
        'k_k': 0.85 + nrm((L, DA), 0.02),
        'k_a': gain((L, DA)),
        'r_k': nrm((L, N_HEADS, HEAD_SIZE), 0.1),
        'ln_x_g': gain((L, DA)),
        'ln_x_b': nrm((L, DA), 0.02),
        'conv_w': nrm((L, CONV_WIDTH, DC), CONV_WIDTH ** -0.5),
        'conv_b': nrm((L, DC), 0.02),
        'conv_ln_g': gain((L, DC)),
        'conv_ln_b': nrm((L, DC), 0.02),
        'w_conv_out': nrm((L, DC, D), DC ** -0.5),
        'w_out': nrm((L, D, D), D ** -0.5),
        'ffn2_norm': gain((L, D)),
        'ffn2_w_gate': nrm((L, D, F), D ** -0.5),
        'ffn2_w_up': nrm((L, D, F), D ** -0.5),
        'ffn2_w_down': nrm((L, F, D), F ** -0.5),
        'final_norm': gain((D,)),
    }


def reference(x_prompt, x_sample, state_wkv, state_shift, state_conv,
              ffn1_norm, ffn1_w_gate, ffn1_w_up, ffn1_w_down, mix_norm, w_in, mu,
              w0, w_decay_up, a0, w_iclr_up, w_gate_up, k_k, k_a, r_k, ln_x_g, ln_x_b,
              conv_w, conv_b, conv_ln_g, conv_ln_b, w_conv_out, w_out,
              ffn2_norm, ffn2_w_gate, ffn2_w_up, ffn2_w_down, final_norm):
    xp, xs = x_prompt, x_sample
    Bp = x_prompt.shape[0]
    sp_shift = jnp.zeros((Bp, D_MODEL), x_prompt.dtype)
    sp_wkv = jnp.zeros((Bp, N_HEADS, HEAD_SIZE, HEAD_SIZE), jnp.float32)
    sp_conv = jnp.zeros((Bp, CONV_WIDTH - 1, D_CONV), x_prompt.dtype)
    wkv_p, shift_p, conv_p, wkv_s, shift_s, conv_s = [], [], [], [], [], []
    for l in range(DEPTH):
        p = {
            'ffn1_norm': ffn1_norm[l], 'ffn1_w_gate': ffn1_w_gate[l], 'ffn1_w_up': ffn1_w_up[l],
            'ffn1_w_down': ffn1_w_down[l], 'mix_norm': mix_norm[l], 'w_in': w_in[l], 'mu': mu[l],
            'w0': w0[l], 'w_decay_up': w_decay_up[l], 'a0': a0[l], 'w_iclr_up': w_iclr_up[l],
            'w_gate_up': w_gate_up[l], 'k_k': k_k[l], 'k_a': k_a[l], 'r_k': r_k[l],
            'ln_x_g': ln_x_g[l], 'ln_x_b': ln_x_b[l], 'conv_w': conv_w[l], 'conv_b': conv_b[l],
            'conv_ln_g': conv_ln_g[l], 'conv_ln_b': conv_ln_b[l], 'w_conv_out': w_conv_out[l],
            'w_out': w_out[l], 'ffn2_norm': ffn2_norm[l], 'ffn2_w_gate': ffn2_w_gate[l],
            'ffn2_w_up': ffn2_w_up[l], 'ffn2_w_down': ffn2_w_down[l],
        }
        xp, sh, S, cb = _layer(xp, sp_shift, sp_wkv, sp_conv, p)
        shift_p.append(sh); wkv_p.append(S.astype(x_prompt.dtype)); conv_p.append(cb)
        xs, sh, S, cb = _layer(xs, state_shift[l], state_wkv[l], state_conv[l], p)
        shift_s.append(sh.astype(state_shift.dtype)); wkv_s.append(S.astype(state_wkv.dtype))
        conv_s.append(cb.astype(state_conv.dtype))
    y_prompt = _rmsnorm(xp, final_norm)
    y_sample = _rmsnorm(xs, final_norm)
    return (y_prompt, y_sample, jnp.stack(wkv_p), jnp.stack(shift_p), jnp.stack(conv_p),
            jnp.stack(wkv_s), jnp.stack(shift_s), jnp.stack(conv_s))
```

```python
import functools
import math

import jax
import jax.numpy as jnp
from jax import lax
from jax.experimental import pallas as pl
from jax.experimental.pallas import tpu as pltpu

F32 = jnp.float32
BF16 = jnp.bfloat16

HEAD = 64
SEG = 256
NORM_EPS = 1e-6
GN_EPS = 64e-5
LN_EPS = 1e-5
VMEM_LIMIT = 56 * 1024 * 1024


def _cparams(sem):
    return pltpu.CompilerParams(dimension_semantics=sem, vmem_limit_bytes=VMEM_LIMIT)


def _dot(a, b):
    return jnp.dot(a.astype(BF16), b.astype(BF16), preferred_element_type=F32)


def _dot_nt(a, b):
    return lax.dot_general(a.astype(BF16), b.astype(BF16), (((1,), (1,)), ((), ())),
                           preferred_element_type=F32)


def _dot_tn(a, b):
    return lax.dot_general(a.astype(BF16), b.astype(BF16), (((0,), (0,)), ((), ())),
                           preferred_element_type=F32)


def _split3(x):
    hi = x.astype(BF16)
    r1 = x - hi.astype(F32)
    mid = r1.astype(BF16)
    lo = (r1 - mid.astype(F32)).astype(BF16)
    return hi, mid, lo


def _dot_exact_rhs(x, m):
    hi, mid, lo = _split3(x)
    d = lambda t: jnp.dot(t, m, preferred_element_type=F32)
    return d(hi) + d(mid) + d(lo)


def _dot_exact_lhs(m, x):
    hi, mid, lo = _split3(x)
    d = lambda t: jnp.dot(m, t, preferred_element_type=F32)
    return d(hi) + d(mid) + d(lo)


def _segsum(x, bo):
    d = x.shape[1]
    parts = [_dot_exact_rhs(x[:, s:s + SEG], bo) for s in range(0, d, SEG)]
    return parts[0] if len(parts) == 1 else jnp.concatenate(parts, axis=1)


def _rmsnorm(x, g):
    return x * lax.rsqrt(jnp.mean(x * x, axis=-1, keepdims=True) + NORM_EPS) * g


def _sigmoid(x):
    return jax.nn.sigmoid(x)


def _pick_tile(n, target, mult=8):
    if n <= target:
        return n
    for t in range(target - target % mult, 0, -mult):
        if n % t == 0:
            return t
    return n


def _ffn_body(x_ref, g_ref, wg_ref, wu_ref, wd_ref, o_ref, hb_ref, acc_ref):
    j = pl.program_id(1)

    @pl.when(j == 0)
    def _():
        hb_ref[...] = _rmsnorm(x_ref[...], g_ref[...]).astype(BF16)
        acc_ref[...] = jnp.zeros_like(acc_ref)

    hb = hb_ref[...]
    gt = jnp.dot(hb, wg_ref[...], preferred_element_type=F32)
    ut = jnp.dot(hb, wu_ref[...], preferred_element_type=F32)
    act = (gt * _sigmoid(gt)) * ut
    acc_ref[...] += jnp.dot(act.astype(BF16), wd_ref[...], preferred_element_type=F32)

    @pl.when(j == pl.num_programs(1) - 1)
    def _():
        o_ref[...] = x_ref[...] + 0.5 * acc_ref[...]


def _ffn(x, g, wg, wu, wd, layer):
    m, d = x.shape
    f = wg.shape[-1]
    tm = _pick_tile(m, 1024)
    tf = _pick_tile(f, 256, 128)
    return pl.pallas_call(
        _ffn_body,
        grid=(m // tm, f // tf),
        in_specs=[
            pl.BlockSpec((tm, d), lambda i, j: (i, 0)),
            pl.BlockSpec((None, 1, d), lambda i, j: (layer, 0, 0)),
            pl.BlockSpec((None, d, tf), lambda i, j: (layer, 0, j)),
            pl.BlockSpec((None, d, tf), lambda i, j: (layer, 0, j)),
            pl.BlockSpec((None, tf, d), lambda i, j: (layer, j, 0)),
        ],
        out_specs=pl.BlockSpec((tm, d), lambda i, j: (i, 0)),
        out_shape=jax.ShapeDtypeStruct((m, d), F32),
        scratch_shapes=[pltpu.VMEM((tm, d), BF16), pltpu.VMEM((tm, d), F32)],
        compiler_params=_cparams(("parallel", "arbitrary")),
        name="ffn",
    )(x, g, wg, wu, wd)


def _mixin_body(x_ref, sp_ref, ng_ref, wa_ref, mu_ref, w0_ref, a0_ref, kk_ref, ka_ref,
                wdec_ref, wiclr_ref, wgate_ref, bo_ref,
                r_ref, lw_ref, kp_ref, v_ref, nkk_ref, b_ref, g_ref, hs_ref,
                carry_ref, *, shift, tiles_per_seq, d, ld, li):
    i = pl.program_id(0)
    tm = x_ref.shape[0]
    h = _rmsnorm(x_ref[...], ng_ref[...])
    hs_ref[...] = h[tm - hs_ref.shape[0]:, :]
    ys = jnp.dot(h.astype(BF16), wa_ref[...], preferred_element_type=F32)

    if shift == 1:
        @pl.when(i % tiles_per_seq == 0)
        def _():
            carry_ref[...] = jnp.zeros_like(carry_ref)

        rows = lax.broadcasted_iota(jnp.int32, ys.shape, 0)
        prev = jnp.where(rows == 0, carry_ref[7:8, :], pltpu.roll(ys, 1, axis=0))
        carry_ref[...] = ys[tm - 8:, :]
    else:
        yp = jnp.dot(sp_ref[...].astype(BF16), wa_ref[...], preferred_element_type=F32)
        prev = jnp.concatenate([yp, ys[:tm - shift, :]], axis=0)

    xs = ys + (prev - ys) * mu_ref[...]
    r = xs[:, :d]
    k = xs[:, d:2 * d]
    v = xs[:, 2 * d:3 * d]
    o2 = 3 * d
    xw = xs[:, o2:o2 + ld]
    xa = xs[:, o2 + ld:o2 + ld + li]
    xg = xs[:, o2 + ld + li:]

    z = -(w0_ref[...] + _dot(jnp.tanh(xw), wdec_ref[...]))
    softplus = jnp.maximum(z, 0.0) + jnp.log(1.0 + jnp.exp(-jnp.abs(z)))
    w_log = -softplus - 0.5
    lw = -jnp.exp(w_log)
    a = _sigmoid(a0_ref[...] + _dot(xa, wiclr_ref[...]))
    g = _dot(_sigmoid(xg), wgate_ref[...])

    kk = k * kk_ref[...]
    nrm = jnp.sqrt(_segsum(kk * kk, bo_ref[...]))
    kk = kk / jnp.maximum(nrm, 1e-12)
    kp = k * (1.0 + (a - 1.0) * ka_ref[...])

    r_ref[...] = r
    lw_ref[...] = lw
    kp_ref[...] = kp
    v_ref[...] = v
    nkk_ref[...] = -kk
    b_ref[...] = kk * a
    g_ref[...] = g


def _mix_in(x, sp, p, layer, *, shift, seq_rows):
    m, d = x.shape
    sw = p['wa'].shape[-1]
    ld = p['wdec'].shape[1]
    li = p['wiclr'].shape[1]
    lg = p['wgate'].shape[1]
    if shift == 1:
        tm = _pick_tile(seq_rows, 256)
        tps = seq_rows // tm
        hs_rows = 8
    else:
        tm = m
        tps = 1
        hs_rows = shift
    nt = m // tm
    vec = lambda: pl.BlockSpec((None, 1, d), lambda i: (layer, 0, 0))
    tile = lambda: pl.BlockSpec((tm, d), lambda i: (i, 0))
    body = functools.partial(_mixin_body, shift=shift, tiles_per_seq=tps, d=d, ld=ld, li=li)
    outs = pl.pallas_call(
        body,
        grid=(nt,),
        in_specs=[
            tile(),
            pl.BlockSpec(sp.shape, lambda i: (0, 0)),
            vec(),
            pl.BlockSpec((None, d, sw), lambda i: (layer, 0, 0)),
            pl.BlockSpec((None, 1, sw), lambda i: (layer, 0, 0)),
            vec(), vec(), vec(), vec(),
            pl.BlockSpec((None, ld, d), lambda i: (layer, 0, 0)),
            pl.BlockSpec((None, li, d), lambda i: (layer, 0, 0)),
            pl.BlockSpec((None, lg, d), lambda i: (layer, 0, 0)),
            pl.BlockSpec((SEG, SEG), lambda i: (0, 0)),
        ],
        out_specs=[tile() for _ in range(7)] + [pl.BlockSpec((None, hs_rows, d), lambda i: (i // tps, 0, 0))],
        out_shape=[jax.ShapeDtypeStruct((m, d), F32) for _ in range(7)]
                  + [jax.ShapeDtypeStruct((nt // tps, hs_rows, d), F32)],
        scratch_shapes=[pltpu.VMEM((8, sw), F32)],
        compiler_params=_cparams(("arbitrary",)),
        name="mix_in",
    )(x, sp, p['mix_norm'], p['wa'], p['mu'], p['w0'], p['a0'], p['k_k'], p['k_a'],
      p['wdec'], p['wiclr'], p['wgate'], p['bo'])
    return outs


def _wkv_body(r_ref, lw_ref, kp_ref, v_ref, a_ref, b_ref, g_ref, s0_ref,
              lng_ref, lnb_ref, rk_ref, bo_ref, tri_ref,
              oa_ref, sout_ref, s_ref, y_ref, *, heads):
    c = pl.program_id(1)
    gb, ch, d = r_ref.shape

    @pl.when(c == 0)
    def _():
        s_ref[...] = s0_ref[...]

    tri = tri_ref[...]
    rows = lax.broadcasted_iota(jnp.int32, (ch, ch), 0)
    cols = lax.broadcasted_iota(jnp.int32, (ch, ch), 1)
    strict = cols < rows
    incl = cols <= rows
    eye = (cols == rows).astype(F32)
    bo = bo_ref[...]

    def one(i, carry):
        lw = lw_ref[i]
        r = r_ref[i]
        kp = kp_ref[i]
        v = v_ref[i]
        cum = _dot_exact_lhs(tri, lw)
        p_in = jnp.exp(cum)
        p_inv = jnp.exp(-cum)
        at = a_ref[i] * jnp.exp(cum - lw)
        rt = r * p_in
        bt = b_ref[i] * p_inv
        kt = kp * p_inv
        pend = p_in[ch - 1:ch, :]
        for hd in range(heads):
            sl = slice(hd * HEAD, (hd + 1) * HEAD)
            ar = jnp.concatenate([at[:, sl], rt[:, sl]], axis=0)
            bth = bt[:, sl]
            kth = kt[:, sl]
            vh = v[:, sl]
            sh = s_ref[i, hd]
            mb = _dot_nt(ar, bth)
            mk = _dot_nt(ar, kth)
            ars = _dot_nt(ar, sh)
            lab = jnp.where(strict, mb[:ch], 0.0)
            lak = jnp.where(strict, mk[:ch], 0.0)
            mrb = jnp.where(incl, mb[ch:], 0.0)
            mrk = jnp.where(incl, mk[ch:], 0.0)
            tinv = eye + lab
            pw = lab
            for _ in range(int(math.log2(ch)) - 1):
                pw = _dot(pw, pw)
                tinv = tinv + _dot(tinv, pw)
            u = _dot(tinv, ars[:ch] + _dot(lak, vh))
            y_ref[:, sl] = ars[ch:] + _dot(mrb, u) + _dot(mrk, vh)
            s_ref[i, hd] = (sh + _dot_tn(u, bth) + _dot_tn(vh, kth)) * pend[:, sl]
        y = y_ref[...]
        mean = _segsum(y, bo) * (1.0 / HEAD)
        dlt = y - mean
        var = _segsum(dlt * dlt, bo) * (1.0 / HEAD)
        gn = dlt * lax.rsqrt(var + GN_EPS) * lng_ref[...] + lnb_ref[...]
        bonus = _segsum(r * kp * rk_ref[...], bo) * v
        oa_ref[i] = (gn + bonus) * g_ref[i]
        return carry

    lax.fori_loop(0, gb, one, 0)

    @pl.when(c == pl.num_programs(1) - 1)
    def _():
        sout_ref[...] = s_ref[...]


def _wkv(r, lw, kp, v, a, b, g, s0, p, layer, *, chunk, gb):
    nb, t, d = r.shape
    heads = d // HEAD
    seq = lambda: pl.BlockSpec((gb, chunk, d), lambda bi, ci: (bi, ci, 0))
    st = lambda: pl.BlockSpec((gb, heads, HEAD, HEAD), lambda bi, ci: (bi, 0, 0, 0))
    vec = lambda: pl.BlockSpec((None, 1, d), lambda bi, ci: (layer, 0, 0))
    tri = jnp.tril(jnp.ones((chunk, chunk), F32)).astype(BF16)
    return pl.pallas_call(
        functools.partial(_wkv_body, heads=heads),
        grid=(nb // gb, t // chunk),
        in_specs=[seq() for _ in range(7)] + [st(), vec(), vec(), vec(),
                  pl.BlockSpec((SEG, SEG), lambda bi, ci: (0, 0)),
                  pl.BlockSpec((chunk, chunk), lambda bi, ci: (0, 0))],
        out_specs=[seq(), st()],
        out_shape=[jax.ShapeDtypeStruct((nb, t, d), F32),
                   jax.ShapeDtypeStruct((nb, heads, HEAD, HEAD), F32)],
        scratch_shapes=[pltpu.VMEM((gb, heads, HEAD, HEAD), F32), pltpu.VMEM((chunk, d), F32)],
        compiler_params=_cparams(("parallel", "arbitrary")),
        name="wkv",
    )(r, lw, kp, v, a, b, g, s0, p['ln_x_g'], p['ln_x_b'], p['r_k'], p['bo'], tri)


def _mixout_body(x_ref, oa_ref, hist_ref, ng_ref, wb_ref, cw_ref, cb_ref, clg_ref, clb_ref,
                 wco_ref, wo_ref, o_ref, nb_ref, up_ref, *, shift, tiles_per_seq, d, dc, taps):
    i = pl.program_id(0)
    tm = x_ref.shape[0]
    hist = hist_ref.shape[0]
    off = hist - (taps - 1) * shift

    @pl.when(i % tiles_per_seq == 0)
    def _():
        up_ref[0:hist, :] = hist_ref[...]

    x = x_ref[...]
    hb = _rmsnorm(x, ng_ref[...]).astype(BF16)
    cg = jnp.dot(hb, wb_ref[:, :2 * dc], preferred_element_type=F32)
    u = cg[:, :dc] * _sigmoid(cg[:, dc:])
    up_ref[hist:hist + tm, :] = u
    dw = jnp.zeros((tm, dc), F32) + cb_ref[...]
    for tau in range(taps):
        dw = dw + up_ref[off + tau * shift:off + tau * shift + tm, :] * cw_ref[tau:tau + 1, :]
    nb_ref[...] = up_ref[tm:tm + hist, :]
    if tiles_per_seq > 1:
        up_ref[0:hist, :] = up_ref[tm:tm + hist, :]

    mu = jnp.mean(dw, axis=-1, keepdims=True)
    dd = dw - mu
    var = jnp.mean(dd * dd, axis=-1, keepdims=True)
    ln = dd * lax.rsqrt(var + LN_EPS) * clg_ref[...] + clb_ref[...]
    ob = _dot(ln * _sigmoid(ln), wco_ref[...])
    ga = _sigmoid(jnp.dot(hb, wb_ref[:, 2 * dc:2 * dc + d], preferred_element_type=F32))
    gb = _sigmoid(jnp.dot(hb, wb_ref[:, 2 * dc + d:], preferred_element_type=F32))
    z = ga * oa_ref[...] + gb * ob
    o_ref[...] = x + _dot(z, wo_ref[...])


def _mix_out(x, oa, hist, p, layer, *, shift, seq_rows):
    m, d = x.shape
    dc = p['wco'].shape[1]
    taps = p['conv_w'].shape[1]
    wbw = p['wb'].shape[-1]
    hrows = hist.shape[1]
    if shift == 1:
        tm = _pick_tile(seq_rows, 256)
        tps = seq_rows // tm
    else:
        tm = m
        tps = 1
    nt = m // tm
    assert tps == 1 or tm >= hrows
    tile = lambda: pl.BlockSpec((tm, d), lambda i: (i, 0))
    body = functools.partial(_mixout_body, shift=shift, tiles_per_seq=tps, d=d, dc=dc, taps=taps)
    return pl.pallas_call(
        body,
        grid=(nt,),
        in_specs=[
            tile(), tile(),
            pl.BlockSpec((None, hrows, dc), lambda i: (i // tps, 0, 0)),
            pl.BlockSpec((None, 1, d), lambda i: (layer, 0, 0)),
            pl.BlockSpec((None, d, wbw), lambda i: (layer, 0, 0)),
            pl.BlockSpec((None, taps, dc), lambda i: (layer, 0, 0)),
            pl.BlockSpec((None, 1, dc), lambda i: (layer, 0, 0)),
            pl.BlockSpec((None, 1, dc), lambda i: (layer, 0, 0)),
            pl.BlockSpec((None, 1, dc), lambda i: (layer, 0, 0)),
            pl.BlockSpec((None, dc, d), lambda i: (layer, 0, 0)),
            pl.BlockSpec((None, d, d), lambda i: (layer, 0, 0)),
        ],
        out_specs=[tile(), pl.BlockSpec((None, hrows, dc), lambda i: (i // tps, 0, 0))],
        out_shape=[jax.ShapeDtypeStruct((m, d), F32),
                   jax.ShapeDtypeStruct((nt // tps, hrows, dc), F32)],
        scratch_shapes=[pltpu.VMEM((hrows + tm, dc), F32)],
        compiler_params=_cparams(("arbitrary",)),
        name="mix_out",
    )(x, oa, hist, p['mix_norm'], p['wb'], p['conv_w'], p['conv_b'], p['conv_ln_g'],
      p['conv_ln_b'], p['wco'], p['wo'])


def _norm_body(x_ref, g_ref, o_ref):
    o_ref[...] = _rmsnorm(x_ref[...], g_ref[...])


def _final_norm(x, g):
    m, d = x.shape
    tm = _pick_tile(m, 1024)
    return pl.pallas_call(
        _norm_body,
        grid=(m // tm,),
        in_specs=[pl.BlockSpec((tm, d), lambda i: (i, 0)), pl.BlockSpec((1, d), lambda i: (0, 0))],
        out_specs=pl.BlockSpec((tm, d), lambda i: (i, 0)),
        out_shape=jax.ShapeDtypeStruct((m, d), F32),
        compiler_params=_cparams(("parallel",)),
        name="final_norm",
    )(x, g)


def kernel(x_prompt, x_sample, state_wkv, state_shift, state_conv, ffn1_norm, ffn1_w_gate, ffn1_w_up, ffn1_w_down, mix_norm, w_in, mu, w0, w_decay_up, a0, w_iclr_up, w_gate_up, k_k, k_a, r_k, ln_x_g, ln_x_b, conv_w, conv_b, conv_ln_g, conv_ln_b, w_conv_out, w_out, ffn2_norm, ffn2_w_gate, ffn2_w_up, ffn2_w_down, final_norm):
    bp, tp, d = x_prompt.shape
    bs, ts, _ = x_sample.shape
    depth = w_in.shape[0]
    heads = d // HEAD
    dc = conv_w.shape[-1]
    taps = conv_w.shape[1]
    ld, li, lg = w_decay_up.shape[1], w_iclr_up.shape[1], w_gate_up.shape[1]
    sw = 3 * d + ld + li + lg
    row = lambda t: t[:, None, :]
    seg = jnp.arange(SEG) // HEAD
    p = {
        'mix_norm': row(mix_norm), 'mu': row(mu), 'w0': row(w0), 'a0': row(a0), 'k_k': row(k_k),
        'k_a': row(k_a), 'r_k': r_k.reshape(depth, 1, d), 'ln_x_g': row(ln_x_g), 'ln_x_b': row(ln_x_b),
        'conv_w': conv_w, 'conv_b': row(conv_b), 'conv_ln_g': row(conv_ln_g), 'conv_ln_b': row(conv_ln_b),
        'wa': w_in[:, :, :sw].astype(BF16), 'wb': w_in[:, :, sw:].astype(BF16),
        'wdec': w_decay_up.astype(BF16), 'wiclr': w_iclr_up.astype(BF16), 'wgate': w_gate_up.astype(BF16),
        'wco': w_conv_out.astype(BF16), 'wo': w_out.astype(BF16),
        'bo': (seg[:, None] == seg[None, :]).astype(BF16),
    }
    f1 = (row(ffn1_norm), ffn1_w_gate.astype(BF16), ffn1_w_up.astype(BF16), ffn1_w_down.astype(BF16))
    f2 = (row(ffn2_norm), ffn2_w_gate.astype(BF16), ffn2_w_up.astype(BF16), ffn2_w_down.astype(BF16))

    xp = x_prompt.reshape(bp * tp, d)
    xs = jnp.swapaxes(x_sample, 0, 1).reshape(ts * bs, d)
    hist_rows_p = -(-(taps - 1) // 8) * 8
    zeros_sp = jnp.zeros((8, d), F32)
    zeros_hist = jnp.zeros((bp, hist_rows_p, dc), F32)
    zeros_s0 = jnp.zeros((bp, heads, HEAD, HEAD), F32)
    chunk_p = _pick_tile(tp, 64)
    ts_pad = -(-ts // 8) * 8

    def to_bm(t):
        t = jnp.swapaxes(t.reshape(ts, bs, d), 0, 1)
        return jnp.pad(t, ((0, 0), (0, ts_pad - ts), (0, 0)))

    wkv_p, shift_p, conv_p, wkv_s, shift_s, conv_s = [], [], [], [], [], []
    for l in range(depth):
        xp = _ffn(xp, *f1, l)
        *seqs, hs = _mix_in(xp, zeros_sp, p, l, shift=1, seq_rows=tp)
        seqs = [t.reshape(bp, tp, d) for t in seqs]
        oa, s_new = _wkv(*seqs, zeros_s0, p, l, chunk=chunk_p, gb=1)
        xp, nbuf = _mix_out(xp, oa.reshape(bp * tp, d), zeros_hist, p, l, shift=1, seq_rows=tp)
        xp = _ffn(xp, *f2, l)
        wkv_p.append(s_new)
        shift_p.append(hs[:, -1, :])
        conv_p.append(nbuf[:, hist_rows_p - (taps - 1):, :])
        xs = _ffn(xs, *f1, l)
        *seqs, hs = _mix_in(xs, state_shift[l], p, l, shift=bs, seq_rows=ts * bs)
        seqs = [to_bm(t) for t in seqs]
        oa, s_new = _wkv(*seqs, state_wkv[l], p, l, chunk=ts_pad, gb=8)
        oa = jnp.swapaxes(oa[:, :ts, :], 0, 1).reshape(ts * bs, d)
        hist = jnp.swapaxes(state_conv[l], 0, 1).reshape(1, (taps - 1) * bs, dc)
        xs, nbuf = _mix_out(xs, oa, hist, p, l, shift=bs, seq_rows=ts * bs)
        xs = _ffn(xs, *f2, l)
        wkv_s.append(s_new)
        shift_s.append(hs[0])
        conv_s.append(jnp.swapaxes(nbuf.reshape(taps - 1, bs, dc), 0, 1))

    gfin = final_norm[None, :]
    y_prompt = _final_norm(xp, gfin).reshape(bp, tp, d)
    y_sample = jnp.swapaxes(_final_norm(xs, gfin).reshape(ts, bs, d), 0, 1)
    return (y_prompt, y_sample, jnp.stack(wkv_p), jnp.stack(shift_p), jnp.stack(conv_p),
            jnp.stack(wkv_s), jnp.stack(shift_s), jnp.stack(conv_s))
```

```python
import functools
import math

import jax
import jax.numpy as jnp
from jax import lax
from jax.experimental import pallas as pl
from jax.experimental.pallas import tpu as pltpu

F32 = jnp.float32
BF16 = jnp.bfloat16

HEAD = 64
SEG = 256
NORM_EPS = 1e-6
GN_EPS = 64e-5
LN_EPS = 1e-5
VMEM_LIMIT = 56 * 1024 * 1024


def _cparams(sem):
    return pltpu.CompilerParams(dimension_semantics=sem, vmem_limit_bytes=VMEM_LIMIT)


def _dot(a, b):
    return jnp.dot(a.astype(BF16), b.astype(BF16), preferred_element_type=F32)


def _dot_nt(a, b):
    return lax.dot_general(a.astype(BF16), b.astype(BF16), (((1,), (1,)), ((), ())),
                           preferred_element_type=F32)


def _dot_tn(a, b):
    return lax.dot_general(a.astype(BF16), b.astype(BF16), (((0,), (0,)), ((), ())),
                           preferred_element_type=F32)


def _split3(x):
    hi = x.astype(BF16)
    r1 = x - hi.astype(F32)
    mid = r1.astype(BF16)
    lo = (r1 - mid.astype(F32)).astype(BF16)
    return hi, mid, lo


def _dot_exact_rhs(x, m):
    hi, mid, lo = _split3(x)
    d = lambda t: jnp.dot(t, m, preferred_element_type=F32)
    return d(hi) + d(mid) + d(lo)


def _dot_exact_lhs(m, x):
    hi, mid, lo = _split3(x)
    d = lambda t: jnp.dot(m, t, preferred_element_type=F32)
    return d(hi) + d(mid) + d(lo)


def _segsum(x, bo):
    d = x.shape[1]
    parts = [_dot_exact_rhs(x[:, s:s + SEG], bo) for s in range(0, d, SEG)]
    return parts[0] if len(parts) == 1 else jnp.concatenate(parts, axis=1)


def _rmsnorm(x, g):
    return x * lax.rsqrt(jnp.mean(x * x, axis=-1, keepdims=True) + NORM_EPS) * g


def _sigmoid(x):
    return jax.nn.sigmoid(x)


def _pick_tile(n, target, mult=8):
    if n <= target:
        return n
    for t in range(target - target % mult, 0, -mult):
        if n % t == 0:
            return t
    return n


def _ffn_body(x_ref, g_ref, wg_ref, wu_ref, wd_ref, o_ref, hb_ref, acc_ref):
    j = pl.program_id(1)

    @pl.when(j == 0)
    def _():
        hb_ref[...] = _rmsnorm(x_ref[...], g_ref[...]).astype(BF16)
        acc_ref[...] = jnp.zeros_like(acc_ref)

    hb = hb_ref[...]
    gt = jnp.dot(hb, wg_ref[...], preferred_element_type=F32)
    ut = jnp.dot(hb, wu_ref[...], preferred_element_type=F32)
    act = (gt * _sigmoid(gt)) * ut
    acc_ref[...] += jnp.dot(act.astype(BF16), wd_ref[...], preferred_element_type=F32)

    @pl.when(j == pl.num_programs(1) - 1)
    def _():
        o_ref[...] = x_ref[...] + 0.5 * acc_ref[...]


def _ffn(x, g, wg, wu, wd, layer):
    m, d = x.shape
    f = wg.shape[-1]
    tm = _pick_tile(m, 1024)
    tf = _pick_tile(f, 256, 128)
    return pl.pallas_call(
        _ffn_body,
        grid=(m // tm, f // tf),
        in_specs=[
            pl.BlockSpec((tm, d), lambda i, j: (i, 0)),
            pl.BlockSpec((None, 1, d), lambda i, j: (layer, 0, 0)),
            pl.BlockSpec((None, d, tf), lambda i, j: (layer, 0, j)),
            pl.BlockSpec((None, d, tf), lambda i, j: (layer, 0, j)),
            pl.BlockSpec((None, tf, d), lambda i, j: (layer, j, 0)),
        ],
        out_specs=pl.BlockSpec((tm, d), lambda i, j: (i, 0)),
        out_shape=jax.ShapeDtypeStruct((m, d), F32),
        scratch_shapes=[pltpu.VMEM((tm, d), BF16), pltpu.VMEM((tm, d), F32)],
        compiler_params=_cparams(("parallel", "arbitrary")),
        name="ffn",
    )(x, g, wg, wu, wd)


def _mixin_body(x_ref, sp_ref, ng_ref, wa_ref, mu_ref, w0_ref, a0_ref, kk_ref, ka_ref,
                wdec_ref, wiclr_ref, wgate_ref, bo_ref,
                r_ref, lw_ref, kp_ref, v_ref, nkk_ref, b_ref, g_ref, hs_ref,
                carry_ref, *, shift, tiles_per_seq, d, ld, li):
    i = pl.program_id(0)
    tm = x_ref.shape[0]
    h = _rmsnorm(x_ref[...], ng_ref[...])
    hs_ref[...] = h[tm - hs_ref.shape[0]:, :]
    ys = jnp.dot(h.astype(BF16), wa_ref[...], preferred_element_type=F32)

    if shift == 1:
        @pl.when(i % tiles_per_seq == 0)
        def _():
            carry_ref[...] = jnp.zeros_like(carry_ref)

        rows = lax.broadcasted_iota(jnp.int32, ys.shape, 0)
        prev = jnp.where(rows == 0, carry_ref[7:8, :], pltpu.roll(ys, 1, axis=0))
        carry_ref[...] = ys[tm - 8:, :]
    else:
        yp = jnp.dot(sp_ref[...].astype(BF16), wa_ref[...], preferred_element_type=F32)
        prev = jnp.concatenate([yp, ys[:tm - shift, :]], axis=0)

    xs = ys + (prev - ys) * mu_ref[...]
    r = xs[:, :d]
    k = xs[:, d:2 * d]
    v = xs[:, 2 * d:3 * d]
    o2 = 3 * d
    xw = xs[:, o2:o2 + ld]
    xa = xs[:, o2 + ld:o2 + ld + li]
    xg = xs[:, o2 + ld + li:]

    z = -(w0_ref[...] + _dot(jnp.tanh(xw), wdec_ref[...]))
    softplus = jnp.maximum(z, 0.0) + jnp.log(1.0 + jnp.exp(-jnp.abs(z)))
    w_log = -softplus - 0.5
    lw = -jnp.exp(w_log)
    a = _sigmoid(a0_ref[...] + _dot(xa, wiclr_ref[...]))
    g = _dot(_sigmoid(xg), wgate_ref[...])

    kk = k * kk_ref[...]
    nrm = jnp.sqrt(_segsum(kk * kk, bo_ref[...]))
    kk = kk / jnp.maximum(nrm, 1e-12)
    kp = k * (1.0 + (a - 1.0) * ka_ref[...])

    r_ref[...] = r
    lw_ref[...] = lw
    kp_ref[...] = kp
    v_ref[...] = v
    nkk_ref[...] = -kk
    b_ref[...] = kk * a
    g_ref[...] = g


def _mix_in(x, sp, p, layer, *, shift, seq_rows):
    m, d = x.shape
    sw = p['wa'].shape[-1]
    ld = p['wdec'].shape[1]
    li = p['wiclr'].shape[1]
    lg = p['wgate'].shape[1]
    if shift == 1:
        tm = _pick_tile(seq_rows, 256)
        tps = seq_rows // tm
        hs_rows = 8
    else:
        tm = m
        tps = 1
        hs_rows = shift
    nt = m // tm
    vec = lambda: pl.BlockSpec((None, 1, d), lambda i: (layer, 0, 0))
    tile = lambda: pl.BlockSpec((tm, d), lambda i: (i, 0))
    body = functools.partial(_mixin_body, shift=shift, tiles_per_seq=tps, d=d, ld=ld, li=li)
    outs = pl.pallas_call(
        body,
        grid=(nt,),
        in_specs=[
            tile(),
            pl.BlockSpec(sp.shape, lambda i: (0, 0)),
            vec(),
            pl.BlockSpec((None, d, sw), lambda i: (layer, 0, 0)),
            pl.BlockSpec((None, 1, sw), lambda i: (layer, 0, 0)),
            vec(), vec(), vec(), vec(),
            pl.BlockSpec((None, ld, d), lambda i: (layer, 0, 0)),
            pl.BlockSpec((None, li, d), lambda i: (layer, 0, 0)),
            pl.BlockSpec((None, lg, d), lambda i: (layer, 0, 0)),
            pl.BlockSpec((SEG, SEG), lambda i: (0, 0)),
        ],
        out_specs=[tile() for _ in range(7)] + [pl.BlockSpec((None, hs_rows, d), lambda i: (i // tps, 0, 0))],
        out_shape=[jax.ShapeDtypeStruct((m, d), F32) for _ in range(7)]
                  + [jax.ShapeDtypeStruct((nt // tps, hs_rows, d), F32)],
        scratch_shapes=[pltpu.VMEM((8, sw), F32)],
        compiler_params=_cparams(("arbitrary",)),
        name="mix_in",
    )(x, sp, p['mix_norm'], p['wa'], p['mu'], p['w0'], p['a0'], p['k_k'], p['k_a'],
      p['wdec'], p['wiclr'], p['wgate'], p['bo'])
    return outs


def _wkv_body(r_ref, lw_ref, kp_ref, v_ref, a_ref, b_ref, g_ref, s0_ref,
              lng_ref, lnb_ref, rk_ref, bo_ref, tri_ref,
              oa_ref, sout_ref, s_ref, y_ref, *, heads):
    c = pl.program_id(1)
    gb, ch, d = r_ref.shape

    @pl.when(c == 0)
    def _():
        s_ref[...] = s0_ref[...]

    tri = tri_ref[...]
    rows = lax.broadcasted_iota(jnp.int32, (ch, ch), 0)
    cols = lax.broadcasted_iota(jnp.int32, (ch, ch), 1)
    strict = cols < rows
    incl = cols <= rows
    eye = (cols == rows).astype(F32)
    bo = bo_ref[...]

    def one(i, carry):
        lw = lw_ref[i]
        r = r_ref[i]
        kp = kp_ref[i]
        v = v_ref[i]
        cum = _dot_exact_lhs(tri, lw)
        p_in = jnp.exp(cum)
        p_inv = jnp.exp(-cum)
        at = a_ref[i] * jnp.exp(cum - lw)
        rt = r * p_in
        bt = b_ref[i] * p_inv
        kt = kp * p_inv
        pend = p_in[ch - 1:ch, :]
        hr = range(heads)
        sls = [slice(hd * HEAD, (hd + 1) * HEAD) for hd in hr]
        s_old = [s_ref[i, hd] for hd in hr]
        ar = [jnp.concatenate([at[:, sl], rt[:, sl]], axis=0).astype(BF16) for sl in sls]
        bth = [bt[:, sl].astype(BF16) for sl in sls]
        kth = [kt[:, sl].astype(BF16) for sl in sls]
        vh = [v[:, sl].astype(BF16) for sl in sls]
        mb = [_dot_nt(ar[hd], bth[hd]) for hd in hr]
        mk = [_dot_nt(ar[hd], kth[hd]) for hd in hr]
        ars = [_dot_nt(ar[hd], s_old[hd]) for hd in hr]
        lab = [jnp.where(strict, m[:ch], 0.0) for m in mb]
        lak = [jnp.where(strict, m[:ch], 0.0) for m in mk]
        mrb = [jnp.where(incl, m[ch:], 0.0) for m in mb]
        mrk = [jnp.where(incl, m[ch:], 0.0) for m in mk]
        tinv = [eye + l for l in lab]
        pw = lab
        for _ in range(int(math.log2(ch)) - 1):
            pw = [_dot(m, m) for m in pw]
            tinv = [t + _dot(t, m) for t, m in zip(tinv, pw)]
        rhs = [ars[hd][:ch] + _dot(lak[hd], vh[hd]) for hd in hr]
        u = [_dot(tinv[hd], rhs[hd]) for hd in hr]
        ys = [ars[hd][ch:] + _dot(mrb[hd], u[hd]) + _dot(mrk[hd], vh[hd]) for hd in hr]
        s_new = [(s_old[hd] + _dot_tn(u[hd], bth[hd]) + _dot_tn(vh[hd], kth[hd])) * pend[:, sls[hd]]
                 for hd in hr]
        for hd in hr:
            y_ref[:, sls[hd]] = ys[hd]
            s_ref[i, hd] = s_new[hd]
        y = y_ref[...]
        mean = _segsum(y, bo) * (1.0 / HEAD)
        dlt = y - mean
        var = _segsum(dlt * dlt, bo) * (1.0 / HEAD)
        gn = dlt * lax.rsqrt(var + GN_EPS) * lng_ref[...] + lnb_ref[...]
        bonus = _segsum(r * kp * rk_ref[...], bo) * v
        oa_ref[i] = (gn + bonus) * g_ref[i]
        return carry

    lax.fori_loop(0, gb, one, 0)

    @pl.when(c == pl.num_programs(1) - 1)
    def _():
        sout_ref[...] = s_ref[...]


def _wkv(r, lw, kp, v, a, b, g, s0, p, layer, *, chunk, gb):
    nb, t, d = r.shape
    heads = d // HEAD
    seq = lambda: pl.BlockSpec((gb, chunk, d), lambda bi, ci: (bi, ci, 0))
    st = lambda: pl.BlockSpec((gb, heads, HEAD, HEAD), lambda bi, ci: (bi, 0, 0, 0))
    vec = lambda: pl.BlockSpec((None, 1, d), lambda bi, ci: (layer, 0, 0))
    tri = jnp.tril(jnp.ones((chunk, chunk), F32)).astype(BF16)
    return pl.pallas_call(
        functools.partial(_wkv_body, heads=heads),
        grid=(nb // gb, t // chunk),
        in_specs=[seq() for _ in range(7)] + [st(), vec(), vec(), vec(),
                  pl.BlockSpec((SEG, SEG), lambda bi, ci: (0, 0)),
                  pl.BlockSpec((chunk, chunk), lambda bi, ci: (0, 0))],
        out_specs=[seq(), st()],
        out_shape=[jax.ShapeDtypeStruct((nb, t, d), F32),
                   jax.ShapeDtypeStruct((nb, heads, HEAD, HEAD), F32)],
        scratch_shapes=[pltpu.VMEM((gb, heads, HEAD, HEAD), F32), pltpu.VMEM((chunk, d), F32)],
        compiler_params=_cparams(("parallel", "arbitrary")),
        name="wkv",
    )(r, lw, kp, v, a, b, g, s0, p['ln_x_g'], p['ln_x_b'], p['r_k'], p['bo'], tri)


def _mixout_body(x_ref, oa_ref, hist_ref, ng_ref, wb_ref, cw_ref, cb_ref, clg_ref, clb_ref,
                 wco_ref, wo_ref, o_ref, nb_ref, up_ref, *, shift, tiles_per_seq, d, dc, taps):
    i = pl.program_id(0)
    tm = x_ref.shape[0]
    hist = hist_ref.shape[0]
    off = hist - (taps - 1) * shift

    @pl.when(i % tiles_per_seq == 0)
    def _():
        up_ref[0:hist, :] = hist_ref[...]

    x = x_ref[...]
    hb = _rmsnorm(x, ng_ref[...]).astype(BF16)
    cg = jnp.dot(hb, wb_ref[:, :2 * dc], preferred_element_type=F32)
    u = cg[:, :dc] * _sigmoid(cg[:, dc:])
    up_ref[hist:hist + tm, :] = u
    dw = jnp.zeros((tm, dc), F32) + cb_ref[...]
    for tau in range(taps):
        dw = dw + up_ref[off + tau * shift:off + tau * shift + tm, :] * cw_ref[tau:tau + 1, :]
    nb_ref[...] = up_ref[tm:tm + hist, :]
    if tiles_per_seq > 1:
        up_ref[0:hist, :] = up_ref[tm:tm + hist, :]

    mu = jnp.mean(dw, axis=-1, keepdims=True)
    dd = dw - mu
    var = jnp.mean(dd * dd, axis=-1, keepdims=True)
    ln = dd * lax.rsqrt(var + LN_EPS) * clg_ref[...] + clb_ref[...]
    ob = _dot(ln * _sigmoid(ln), wco_ref[...])
    ga = _sigmoid(jnp.dot(hb, wb_ref[:, 2 * dc:2 * dc + d], preferred_element_type=F32))
    gb = _sigmoid(jnp.dot(hb, wb_ref[:, 2 * dc + d:], preferred_element_type=F32))
    z = ga * oa_ref[...] + gb * ob
    o_ref[...] = x + _dot(z, wo_ref[...])


def _mix_out(x, oa, hist, p, layer, *, shift, seq_rows):
    m, d = x.shape
    dc = p['wco'].shape[1]
    taps = p['conv_w'].shape[1]
    wbw = p['wb'].shape[-1]
    hrows = hist.shape[1]
    if shift == 1:
        tm = _pick_tile(seq_rows, 256)
        tps = seq_rows // tm
    else:
        tm = m
        tps = 1
    nt = m // tm
    assert tps == 1 or tm >= hrows
    tile = lambda: pl.BlockSpec((tm, d), lambda i: (i, 0))
    body = functools.partial(_mixout_body, shift=shift, tiles_per_seq=tps, d=d, dc=dc, taps=taps)
    return pl.pallas_call(
        body,
        grid=(nt,),
        in_specs=[
            tile(), tile(),
            pl.BlockSpec((None, hrows, dc), lambda i: (i // tps, 0, 0)),
            pl.BlockSpec((None, 1, d), lambda i: (layer, 0, 0)),
            pl.BlockSpec((None, d, wbw), lambda i: (layer, 0, 0)),
            pl.BlockSpec((None, taps, dc), lambda i: (layer, 0, 0)),
            pl.BlockSpec((None, 1, dc), lambda i: (layer, 0, 0)),
            pl.BlockSpec((None, 1, dc), lambda i: (layer, 0, 0)),
            pl.BlockSpec((None, 1, dc), lambda i: (layer, 0, 0)),
            pl.BlockSpec((None, dc, d), lambda i: (layer, 0, 0)),
            pl.BlockSpec((None, d, d), lambda i: (layer, 0, 0)),
        ],
        out_specs=[tile(), pl.BlockSpec((None, hrows, dc), lambda i: (i // tps, 0, 0))],
        out_shape=[jax.ShapeDtypeStruct((m, d), F32),
                   jax.ShapeDtypeStruct((nt // tps, hrows, dc), F32)],
        scratch_shapes=[pltpu.VMEM((hrows + tm, dc), F32)],
        compiler_params=_cparams(("arbitrary",)),
        name="mix_out",
    )(x, oa, hist, p['mix_norm'], p['wb'], p['conv_w'], p['conv_b'], p['conv_ln_g'],
      p['conv_ln_b'], p['wco'], p['wo'])


def _norm_body(x_ref, g_ref, o_ref):
    o_ref[...] = _rmsnorm(x_ref[...], g_ref[...])


def _final_norm(x, g):
    m, d = x.shape
    tm = _pick_tile(m, 1024)
    return pl.pallas_call(
        _norm_body,
        grid=(m // tm,),
        in_specs=[pl.BlockSpec((tm, d), lambda i: (i, 0)), pl.BlockSpec((1, d), lambda i: (0, 0))],
        out_specs=pl.BlockSpec((tm, d), lambda i: (i, 0)),
        out_shape=jax.ShapeDtypeStruct((m, d), F32),
        compiler_params=_cparams(("parallel",)),
        name="final_norm",
    )(x, g)


def kernel(x_prompt, x_sample, state_wkv, state_shift, state_conv, ffn1_norm, ffn1_w_gate, ffn1_w_up, ffn1_w_down, mix_norm, w_in, mu, w0, w_decay_up, a0, w_iclr_up, w_gate_up, k_k, k_a, r_k, ln_x_g, ln_x_b, conv_w, conv_b, conv_ln_g, conv_ln_b, w_conv_out, w_out, ffn2_norm, ffn2_w_gate, ffn2_w_up, ffn2_w_down, final_norm):
    bp, tp, d = x_prompt.shape
    bs, ts, _ = x_sample.shape
    depth = w_in.shape[0]
    heads = d // HEAD
    dc = conv_w.shape[-1]
    taps = conv_w.shape[1]
    ld, li, lg = w_decay_up.shape[1], w_iclr_up.shape[1], w_gate_up.shape[1]
    sw = 3 * d + ld + li + lg
    row = lambda t: t[:, None, :]
    seg = jnp.arange(SEG) // HEAD
    p = {
        'mix_norm': row(mix_norm), 'mu': row(mu), 'w0': row(w0), 'a0': row(a0), 'k_k': row(k_k),
        'k_a': row(k_a), 'r_k': r_k.reshape(depth, 1, d), 'ln_x_g': row(ln_x_g), 'ln_x_b': row(ln_x_b),
        'conv_w': conv_w, 'conv_b': row(conv_b), 'conv_ln_g': row(conv_ln_g), 'conv_ln_b': row(conv_ln_b),
        'wa': w_in[:, :, :sw].astype(BF16), 'wb': w_in[:, :, sw:].astype(BF16),
        'wdec': w_decay_up.astype(BF16), 'wiclr': w_iclr_up.astype(BF16), 'wgate': w_gate_up.astype(BF16),
        'wco': w_conv_out.astype(BF16), 'wo': w_out.astype(BF16),
        'bo': (seg[:, None] == seg[None, :]).astype(BF16),
    }
    f1 = (row(ffn1_norm), ffn1_w_gate.astype(BF16), ffn1_w_up.astype(BF16), ffn1_w_down.astype(BF16))
    f2 = (row(ffn2_norm), ffn2_w_gate.astype(BF16), ffn2_w_up.astype(BF16), ffn2_w_down.astype(BF16))

    xp = x_prompt.reshape(bp * tp, d)
    xs = jnp.swapaxes(x_sample, 0, 1).reshape(ts * bs, d)
    hist_rows_p = -(-(taps - 1) // 8) * 8
    zeros_sp = jnp.zeros((8, d), F32)
    zeros_hist = jnp.zeros((bp, hist_rows_p, dc), F32)
    zeros_s0 = jnp.zeros((bp, heads, HEAD, HEAD), F32)
    chunk_p = _pick_tile(tp, 64)
    ts_pad = -(-ts // 8) * 8

    def to_bm(t):
        t = jnp.swapaxes(t.reshape(ts, bs, d), 0, 1)
        return jnp.pad(t, ((0, 0), (0, ts_pad - ts), (0, 0)))

    wkv_p, shift_p, conv_p, wkv_s, shift_s, conv_s = [], [], [], [], [], []
    for l in range(depth):
        xp = _ffn(xp, *f1, l)
        *seqs, hs = _mix_in(xp, zeros_sp, p, l, shift=1, seq_rows=tp)
        seqs = [t.reshape(bp, tp, d) for t in seqs]
        oa, s_new = _wkv(*seqs, zeros_s0, p, l, chunk=chunk_p, gb=1)
        xp, nbuf = _mix_out(xp, oa.reshape(bp * tp, d), zeros_hist, p, l, shift=1, seq_rows=tp)
        xp = _ffn(xp, *f2, l)
        wkv_p.append(s_new)
        shift_p.append(hs[:, -1, :])
        conv_p.append(nbuf[:, hist_rows_p - (taps - 1):, :])
        xs = _ffn(xs, *f1, l)
        *seqs, hs = _mix_in(xs, state_shift[l], p, l, shift=bs, seq_rows=ts * bs)
        seqs = [to_bm(t) for t in seqs]
        oa, s_new = _wkv(*seqs, state_wkv[l], p, l, chunk=ts_pad, gb=8)
        oa = jnp.swapaxes(oa[:, :ts, :], 0, 1).reshape(ts * bs, d)
        hist = jnp.swapaxes(state_conv[l], 0, 1).reshape(1, (taps - 1) * bs, dc)
        xs, nbuf = _mix_out(xs, oa, hist, p, l, shift=bs, seq_rows=ts * bs)
        xs = _ffn(xs, *f2, l)
        wkv_s.append(s_new)
        shift_s.append(hs[0])
        conv_s.append(jnp.swapaxes(nbuf.reshape(taps - 1, bs, dc), 0, 1))

    gfin = final_norm[None, :]
    y_prompt = _final_norm(xp, gfin).reshape(bp, tp, d)
    y_sample = jnp.swapaxes(_final_norm(xs, gfin).reshape(ts, bs, d), 0, 1)
    return (y_prompt, y_sample, jnp.stack(wkv_p), jnp.stack(shift_p), jnp.stack(conv_p),
            jnp.stack(wkv_s), jnp.stack(shift_s), jnp.stack(conv_s))
```

```python
import functools
import math

import jax
import jax.numpy as jnp
from jax import lax
from jax.experimental import pallas as pl
from jax.experimental.pallas import tpu as pltpu

F32 = jnp.float32
BF16 = jnp.bfloat16

HEAD = 64
SEG = 256
NORM_EPS = 1e-6
GN_EPS = 64e-5
LN_EPS = 1e-5
VMEM_LIMIT = 56 * 1024 * 1024


def _cparams(sem):
    return pltpu.CompilerParams(dimension_semantics=sem, vmem_limit_bytes=VMEM_LIMIT)


def _dot(a, b):
    return jnp.dot(a.astype(BF16), b.astype(BF16), preferred_element_type=F32)


def _dot_nt(a, b):
    return lax.dot_general(a.astype(BF16), b.astype(BF16), (((1,), (1,)), ((), ())),
                           preferred_element_type=F32)


def _dot_tn(a, b):
    return lax.dot_general(a.astype(BF16), b.astype(BF16), (((0,), (0,)), ((), ())),
                           preferred_element_type=F32)


def _split3(x):
    hi = x.astype(BF16)
    r1 = x - hi.astype(F32)
    mid = r1.astype(BF16)
    lo = (r1 - mid.astype(F32)).astype(BF16)
    return hi, mid, lo


def _dot_exact_rhs(x, m):
    hi, mid, lo = _split3(x)
    d = lambda t: jnp.dot(t, m, preferred_element_type=F32)
    return d(hi) + d(mid) + d(lo)


def _dot_exact_lhs(m, x):
    hi, mid, lo = _split3(x)
    d = lambda t: jnp.dot(m, t, preferred_element_type=F32)
    return d(hi) + d(mid) + d(lo)


def _segsum(x, bo):
    d = x.shape[1]
    parts = [_dot_exact_rhs(x[:, s:s + SEG], bo) for s in range(0, d, SEG)]
    return parts[0] if len(parts) == 1 else jnp.concatenate(parts, axis=1)


def _rmsnorm(x, g):
    return x * lax.rsqrt(jnp.mean(x * x, axis=-1, keepdims=True) + NORM_EPS) * g


def _sigmoid(x):
    return jax.nn.sigmoid(x)


def _pick_tile(n, target, mult=8):
    if n <= target:
        return n
    for t in range(target - target % mult, 0, -mult):
        if n % t == 0:
            return t
    return n


def _ffn_body(x_ref, g_ref, wg_ref, wu_ref, wd_ref, o_ref, hb_ref, acc_ref):
    j = pl.program_id(1)

    @pl.when(j == 0)
    def _():
        hb_ref[...] = _rmsnorm(x_ref[...], g_ref[...]).astype(BF16)
        acc_ref[...] = jnp.zeros_like(acc_ref)

    hb = hb_ref[...]
    gt = jnp.dot(hb, wg_ref[...], preferred_element_type=F32)
    ut = jnp.dot(hb, wu_ref[...], preferred_element_type=F32)
    act = (gt * _sigmoid(gt)) * ut
    acc_ref[...] += jnp.dot(act.astype(BF16), wd_ref[...], preferred_element_type=F32)

    @pl.when(j == pl.num_programs(1) - 1)
    def _():
        o_ref[...] = x_ref[...] + 0.5 * acc_ref[...]


def _ffn(x, g, wg, wu, wd, layer):
    m, d = x.shape
    f = wg.shape[-1]
    tm = _pick_tile(m, 1024)
    tf = _pick_tile(f, 256, 128)
    return pl.pallas_call(
        _ffn_body,
        grid=(m // tm, f // tf),
        in_specs=[
            pl.BlockSpec((tm, d), lambda i, j: (i, 0)),
            pl.BlockSpec((None, 1, d), lambda i, j: (layer, 0, 0)),
            pl.BlockSpec((None, d, tf), lambda i, j: (layer, 0, j)),
            pl.BlockSpec((None, d, tf), lambda i, j: (layer, 0, j)),
            pl.BlockSpec((None, tf, d), lambda i, j: (layer, j, 0)),
        ],
        out_specs=pl.BlockSpec((tm, d), lambda i, j: (i, 0)),
        out_shape=jax.ShapeDtypeStruct((m, d), F32),
        scratch_shapes=[pltpu.VMEM((tm, d), BF16), pltpu.VMEM((tm, d), F32)],
        compiler_params=_cparams(("parallel", "arbitrary")),
        name="ffn",
    )(x, g, wg, wu, wd)


def _mixin_body(x_ref, sp_ref, ng_ref, wa_ref, mu_ref, w0_ref, a0_ref, kk_ref, ka_ref,
                wdec_ref, wiclr_ref, wgate_ref, bo_ref,
                r_ref, lw_ref, kp_ref, v_ref, nkk_ref, b_ref, g_ref, hs_ref,
                carry_ref, *, shift, tiles_per_seq, d, ld, li):
    i = pl.program_id(0)
    tm = x_ref.shape[0]
    h = _rmsnorm(x_ref[...], ng_ref[...])
    hs_ref[...] = h[tm - hs_ref.shape[0]:, :]
    ys = jnp.dot(h.astype(BF16), wa_ref[...], preferred_element_type=F32)

    if shift == 1:
        @pl.when(i % tiles_per_seq == 0)
        def _():
            carry_ref[...] = jnp.zeros_like(carry_ref)

        rows = lax.broadcasted_iota(jnp.int32, ys.shape, 0)
        prev = jnp.where(rows == 0, carry_ref[7:8, :], pltpu.roll(ys, 1, axis=0))
        carry_ref[...] = ys[tm - 8:, :]
    else:
        yp = jnp.dot(sp_ref[...].astype(BF16), wa_ref[...], preferred_element_type=F32)
        prev = jnp.concatenate([yp, ys[:tm - shift, :]], axis=0)

    xs = ys + (prev - ys) * mu_ref[...]
    r = xs[:, :d]
    k = xs[:, d:2 * d]
    v = xs[:, 2 * d:3 * d]
    o2 = 3 * d
    xw = xs[:, o2:o2 + ld]
    xa = xs[:, o2 + ld:o2 + ld + li]
    xg = xs[:, o2 + ld + li:]

    z = -(w0_ref[...] + _dot(jnp.tanh(xw), wdec_ref[...]))
    softplus = jnp.maximum(z, 0.0) + jnp.log(1.0 + jnp.exp(-jnp.abs(z)))
    w_log = -softplus - 0.5
    lw = -jnp.exp(w_log)
    a = _sigmoid(a0_ref[...] + _dot(xa, wiclr_ref[...]))
    g = _dot(_sigmoid(xg), wgate_ref[...])

    kk = k * kk_ref[...]
    nrm = jnp.sqrt(_segsum(kk * kk, bo_ref[...]))
    kk = kk / jnp.maximum(nrm, 1e-12)
    kp = k * (1.0 + (a - 1.0) * ka_ref[...])

    r_ref[...] = r
    lw_ref[...] = lw
    kp_ref[...] = kp
    v_ref[...] = v
    nkk_ref[...] = -kk
    b_ref[...] = kk * a
    g_ref[...] = g


def _mix_in(x, sp, p, layer, *, shift, seq_rows):
    m, d = x.shape
    sw = p['wa'].shape[-1]
    ld = p['wdec'].shape[1]
    li = p['wiclr'].shape[1]
    lg = p['wgate'].shape[1]
    if shift == 1:
        tm = _pick_tile(seq_rows, 256)
        tps = seq_rows // tm
        hs_rows = 8
    else:
        tm = m
        tps = 1
        hs_rows = shift
    nt = m // tm
    vec = lambda: pl.BlockSpec((None, 1, d), lambda i: (layer, 0, 0))
    tile = lambda: pl.BlockSpec((tm, d), lambda i: (i, 0))
    body = functools.partial(_mixin_body, shift=shift, tiles_per_seq=tps, d=d, ld=ld, li=li)
    outs = pl.pallas_call(
        body,
        grid=(nt,),
        in_specs=[
            tile(),
            pl.BlockSpec(sp.shape, lambda i: (0, 0)),
            vec(),
            pl.BlockSpec((None, d, sw), lambda i: (layer, 0, 0)),
            pl.BlockSpec((None, 1, sw), lambda i: (layer, 0, 0)),
            vec(), vec(), vec(), vec(),
            pl.BlockSpec((None, ld, d), lambda i: (layer, 0, 0)),
            pl.BlockSpec((None, li, d), lambda i: (layer, 0, 0)),
            pl.BlockSpec((None, lg, d), lambda i: (layer, 0, 0)),
            pl.BlockSpec((SEG, SEG), lambda i: (0, 0)),
        ],
        out_specs=[tile() for _ in range(7)] + [pl.BlockSpec((None, hs_rows, d), lambda i: (i // tps, 0, 0))],
        out_shape=[jax.ShapeDtypeStruct((m, d), F32) for _ in range(7)]
                  + [jax.ShapeDtypeStruct((nt // tps, hs_rows, d), F32)],
        scratch_shapes=[pltpu.VMEM((8, sw), F32)],
        compiler_params=_cparams(("arbitrary",)),
        name="mix_in",
    )(x, sp, p['mix_norm'], p['wa'], p['mu'], p['w0'], p['a0'], p['k_k'], p['k_a'],
      p['wdec'], p['wiclr'], p['wgate'], p['bo'])
    return outs


def _wkv_body(r_ref, lw_ref, kp_ref, v_ref, a_ref, b_ref, g_ref, s0_ref,
              lng_ref, lnb_ref, rk_ref, bo_ref, tri_ref,
              oa_ref, sout_ref, s_ref, y_ref, *, heads):
    c = pl.program_id(1)
    gb, ch, d = r_ref.shape
    pairs = heads // 2
    pw2 = 2 * HEAD

    @pl.when(c == 0)
    def _():
        for i in range(gb):
            for p in range(pairs):
                s_ref[i, p] = jnp.concatenate([s0_ref[i, 2 * p], s0_ref[i, 2 * p + 1]], axis=1)

    tri = tri_ref[...]
    left_v = lax.broadcasted_iota(jnp.int32, (1, pw2), 1) < HEAD
    rows = lax.broadcasted_iota(jnp.int32, (ch, 2 * ch), 0)
    cols = lax.broadcasted_iota(jnp.int32, (ch, 2 * ch), 1)
    left_c = cols < ch
    cin = jnp.where(left_c, cols, cols - ch)
    strict = cin < rows
    incl = cin <= rows
    eye = (cin == rows).astype(F32)
    bo = bo_ref[...]

    def bdiag(x, left):
        return jnp.concatenate([jnp.where(left, x, 0.0), jnp.where(left, 0.0, x)], axis=0).astype(BF16)

    seqs = []
    for i in range(gb):
        lw = lw_ref[i]
        cum = _dot_exact_lhs(tri, lw)
        p_in = jnp.exp(cum)
        p_inv = jnp.exp(-cum)
        seqs.append(dict(at=a_ref[i] * jnp.exp(cum - lw), rt=r_ref[i] * p_in, bt=b_ref[i] * p_inv,
                         kt=kp_ref[i] * p_inv, v=v_ref[i], pend=p_in[ch - 1:ch, :]))

    units = [(i, p) for i in range(gb) for p in range(pairs)]
    ur = range(len(units))
    sls = [slice(p * pw2, (p + 1) * pw2) for _, p in units]
    cut = lambda name: [seqs[i][name][:, sl] for (i, _), sl in zip(units, sls)]
    atp, rtp, btp, ktp, vp = cut('at'), cut('rt'), cut('bt'), cut('kt'), cut('v')
    s_old = [s_ref[i, p] for i, p in units]
    ar = [jnp.concatenate([atp[n], rtp[n]], axis=0).astype(BF16) for n in ur]
    mb = [_dot_nt(ar[n], bdiag(btp[n], left_v)) for n in ur]
    mk = [_dot_nt(ar[n], bdiag(ktp[n], left_v)) for n in ur]
    ars = [_dot_nt(ar[n], bdiag(s_old[n], left_v)) for n in ur]
    lab = [jnp.where(strict, m[:ch], 0.0) for m in mb]
    lak = [jnp.where(strict, m[:ch], 0.0) for m in mk]
    mrb = [jnp.where(incl, m[ch:], 0.0) for m in mb]
    mrk = [jnp.where(incl, m[ch:], 0.0) for m in mk]
    tinv = [eye + l for l in lab]
    pw = [_dot(m, bdiag(m, left_c)) for m in lab]
    for _ in range(int(math.log2(ch)) - 2):
        stk = [_dot(jnp.concatenate([t, m], axis=0), bdiag(m, left_c)) for t, m in zip(tinv, pw)]
        tinv = [t + s[:ch] for t, s in zip(tinv, stk)]
        pw = [s[ch:] for s in stk]
    tinv = [t + _dot(t, bdiag(m, left_c)) for t, m in zip(tinv, pw)]
    kv = [_dot(jnp.concatenate([lak[n], mrk[n]], axis=0), bdiag(vp[n], left_v)) for n in ur]
    rhs = [ars[n][:ch] + kv[n][:ch] for n in ur]
    u = [_dot(tinv[n], bdiag(rhs[n], left_v)) for n in ur]
    ys = [ars[n][ch:] + _dot(mrb[n], bdiag(u[n], left_v)) + kv[n][ch:] for n in ur]
    full = [_dot_tn(jnp.concatenate([u[n], vp[n]], axis=0),
                    jnp.concatenate([btp[n], ktp[n]], axis=0)) for n in ur]
    for n, (i, p) in enumerate(units):
        upd = jnp.where(left_v, full[n][:HEAD], full[n][HEAD:])
        y_ref[i, :, sls[n]] = ys[n]
        s_ref[i, p] = (s_old[n] + upd) * seqs[i]['pend'][:, sls[n]]

    for i in range(gb):
        y = y_ref[i]
        mean = _segsum(y, bo) * (1.0 / HEAD)
        dlt = y - mean
        var = _segsum(dlt * dlt, bo) * (1.0 / HEAD)
        gn = dlt * lax.rsqrt(var + GN_EPS) * lng_ref[...] + lnb_ref[...]
        bonus = _segsum(r_ref[i] * kp_ref[i] * rk_ref[...], bo) * v_ref[i]
        oa_ref[i] = (gn + bonus) * g_ref[i]

    @pl.when(c == pl.num_programs(1) - 1)
    def _():
        for i in range(gb):
            for p in range(pairs):
                sp = s_ref[i, p]
                sout_ref[i, 2 * p] = sp[:, :HEAD]
                sout_ref[i, 2 * p + 1] = sp[:, HEAD:]


def _wkv(r, lw, kp, v, a, b, g, s0, p, layer, *, chunk, gb):
    nb, t, d = r.shape
    heads = d // HEAD
    seq = lambda: pl.BlockSpec((gb, chunk, d), lambda bi, ci: (bi, ci, 0))
    st = lambda: pl.BlockSpec((gb, heads, HEAD, HEAD), lambda bi, ci: (bi, 0, 0, 0))
    vec = lambda: pl.BlockSpec((None, 1, d), lambda bi, ci: (layer, 0, 0))
    tri = jnp.tril(jnp.ones((chunk, chunk), F32)).astype(BF16)
    return pl.pallas_call(
        functools.partial(_wkv_body, heads=heads),
        grid=(nb // gb, t // chunk),
        in_specs=[seq() for _ in range(7)] + [st(), vec(), vec(), vec(),
                  pl.BlockSpec((SEG, SEG), lambda bi, ci: (0, 0)),
                  pl.BlockSpec((chunk, chunk), lambda bi, ci: (0, 0))],
        out_specs=[seq(), st()],
        out_shape=[jax.ShapeDtypeStruct((nb, t, d), F32),
                   jax.ShapeDtypeStruct((nb, heads, HEAD, HEAD), F32)],
        scratch_shapes=[pltpu.VMEM((gb, heads // 2, HEAD, 2 * HEAD), F32), pltpu.VMEM((gb, chunk, d), F32)],
        compiler_params=_cparams(("parallel", "arbitrary")),
        name="wkv",
    )(r, lw, kp, v, a, b, g, s0, p['ln_x_g'], p['ln_x_b'], p['r_k'], p['bo'], tri)


def _mixout_body(x_ref, oa_ref, hist_ref, ng_ref, wb_ref, cw_ref, cb_ref, clg_ref, clb_ref,
                 wco_ref, wo_ref, o_ref, nb_ref, up_ref, *, shift, tiles_per_seq, d, dc, taps):
    i = pl.program_id(0)
    tm = x_ref.shape[0]
    hist = hist_ref.shape[0]
    off = hist - (taps - 1) * shift

    @pl.when(i % tiles_per_seq == 0)
    def _():
        up_ref[0:hist, :] = hist_ref[...]

    x = x_ref[...]
    hb = _rmsnorm(x, ng_ref[...]).astype(BF16)
    cg = jnp.dot(hb, wb_ref[:, :2 * dc], preferred_element_type=F32)
    u = cg[:, :dc] * _sigmoid(cg[:, dc:])
    up_ref[hist:hist + tm, :] = u
    dw = jnp.zeros((tm, dc), F32) + cb_ref[...]
    for tau in range(taps):
        dw = dw + up_ref[off + tau * shift:off + tau * shift + tm, :] * cw_ref[tau:tau + 1, :]
    nb_ref[...] = up_ref[tm:tm + hist, :]
    if tiles_per_seq > 1:
        up_ref[0:hist, :] = up_ref[tm:tm + hist, :]

    mu = jnp.mean(dw, axis=-1, keepdims=True)
    dd = dw - mu
    var = jnp.mean(dd * dd, axis=-1, keepdims=True)
    ln = dd * lax.rsqrt(var + LN_EPS) * clg_ref[...] + clb_ref[...]
    ob = _dot(ln * _sigmoid(ln), wco_ref[...])
    ga = _sigmoid(jnp.dot(hb, wb_ref[:, 2 * dc:2 * dc + d], preferred_element_type=F32))
    gb = _sigmoid(jnp.dot(hb, wb_ref[:, 2 * dc + d:], preferred_element_type=F32))
    z = ga * oa_ref[...] + gb * ob
    o_ref[...] = x + _dot(z, wo_ref[...])


def _mix_out(x, oa, hist, p, layer, *, shift, seq_rows):
    m, d = x.shape
    dc = p['wco'].shape[1]
    taps = p['conv_w'].shape[1]
    wbw = p['wb'].shape[-1]
    hrows = hist.shape[1]
    if shift == 1:
        tm = _pick_tile(seq_rows, 256)
        tps = seq_rows // tm
    else:
        tm = m
        tps = 1
    nt = m // tm
    assert tps == 1 or tm >= hrows
    tile = lambda: pl.BlockSpec((tm, d), lambda i: (i, 0))
    body = functools.partial(_mixout_body, shift=shift, tiles_per_seq=tps, d=d, dc=dc, taps=taps)
    return pl.pallas_call(
        body,
        grid=(nt,),
        in_specs=[
            tile(), tile(),
            pl.BlockSpec((None, hrows, dc), lambda i: (i // tps, 0, 0)),
            pl.BlockSpec((None, 1, d), lambda i: (layer, 0, 0)),
            pl.BlockSpec((None, d, wbw), lambda i: (layer, 0, 0)),
            pl.BlockSpec((None, taps, dc), lambda i: (layer, 0, 0)),
            pl.BlockSpec((None, 1, dc), lambda i: (layer, 0, 0)),
            pl.BlockSpec((None, 1, dc), lambda i: (layer, 0, 0)),
            pl.BlockSpec((None, 1, dc), lambda i: (layer, 0, 0)),
            pl.BlockSpec((None, dc, d), lambda i: (layer, 0, 0)),
            pl.BlockSpec((None, d, d), lambda i: (layer, 0, 0)),
        ],
        out_specs=[tile(), pl.BlockSpec((None, hrows, dc), lambda i: (i // tps, 0, 0))],
        out_shape=[jax.ShapeDtypeStruct((m, d), F32),
                   jax.ShapeDtypeStruct((nt // tps, hrows, dc), F32)],
        scratch_shapes=[pltpu.VMEM((hrows + tm, dc), F32)],
        compiler_params=_cparams(("arbitrary",)),
        name="mix_out",
    )(x, oa, hist, p['mix_norm'], p['wb'], p['conv_w'], p['conv_b'], p['conv_ln_g'],
      p['conv_ln_b'], p['wco'], p['wo'])


def _norm_body(x_ref, g_ref, o_ref):
    o_ref[...] = _rmsnorm(x_ref[...], g_ref[...])


def _final_norm(x, g):
    m, d = x.shape
    tm = _pick_tile(m, 1024)
    return pl.pallas_call(
        _norm_body,
        grid=(m // tm,),
        in_specs=[pl.BlockSpec((tm, d), lambda i: (i, 0)), pl.BlockSpec((1, d), lambda i: (0, 0))],
        out_specs=pl.BlockSpec((tm, d), lambda i: (i, 0)),
        out_shape=jax.ShapeDtypeStruct((m, d), F32),
        compiler_params=_cparams(("parallel",)),
        name="final_norm",
    )(x, g)


def kernel(x_prompt, x_sample, state_wkv, state_shift, state_conv, ffn1_norm, ffn1_w_gate, ffn1_w_up, ffn1_w_down, mix_norm, w_in, mu, w0, w_decay_up, a0, w_iclr_up, w_gate_up, k_k, k_a, r_k, ln_x_g, ln_x_b, conv_w, conv_b, conv_ln_g, conv_ln_b, w_conv_out, w_out, ffn2_norm, ffn2_w_gate, ffn2_w_up, ffn2_w_down, final_norm):
    bp, tp, d = x_prompt.shape
    bs, ts, _ = x_sample.shape
    depth = w_in.shape[0]
    heads = d // HEAD
    dc = conv_w.shape[-1]
    taps = conv_w.shape[1]
    ld, li, lg = w_decay_up.shape[1], w_iclr_up.shape[1], w_gate_up.shape[1]
    sw = 3 * d + ld + li + lg
    row = lambda t: t[:, None, :]
    seg = jnp.arange(SEG) // HEAD
    p = {
        'mix_norm': row(mix_norm), 'mu': row(mu), 'w0': row(w0), 'a0': row(a0), 'k_k': row(k_k),
        'k_a': row(k_a), 'r_k': r_k.reshape(depth, 1, d), 'ln_x_g': row(ln_x_g), 'ln_x_b': row(ln_x_b),
        'conv_w': conv_w, 'conv_b': row(conv_b), 'conv_ln_g': row(conv_ln_g), 'conv_ln_b': row(conv_ln_b),
        'wa': w_in[:, :, :sw].astype(BF16), 'wb': w_in[:, :, sw:].astype(BF16),
        'wdec': w_decay_up.astype(BF16), 'wiclr': w_iclr_up.astype(BF16), 'wgate': w_gate_up.astype(BF16),
        'wco': w_conv_out.astype(BF16), 'wo': w_out.astype(BF16),
        'bo': (seg[:, None] == seg[None, :]).astype(BF16),
    }
    f1 = (row(ffn1_norm), ffn1_w_gate.astype(BF16), ffn1_w_up.astype(BF16), ffn1_w_down.astype(BF16))
    f2 = (row(ffn2_norm), ffn2_w_gate.astype(BF16), ffn2_w_up.astype(BF16), ffn2_w_down.astype(BF16))

    xp = x_prompt.reshape(bp * tp, d)
    xs = jnp.swapaxes(x_sample, 0, 1).reshape(ts * bs, d)
    hist_rows_p = -(-(taps - 1) // 8) * 8
    zeros_sp = jnp.zeros((8, d), F32)
    zeros_hist = jnp.zeros((bp, hist_rows_p, dc), F32)
    zeros_s0 = jnp.zeros((bp, heads, HEAD, HEAD), F32)
    chunk_p = _pick_tile(tp, 64)
    ts_pad = -(-ts // 8) * 8

    def to_bm(t):
        t = jnp.swapaxes(t.reshape(ts, bs, d), 0, 1)
        return jnp.pad(t, ((0, 0), (0, ts_pad - ts), (0, 0)))

    wkv_p, shift_p, conv_p, wkv_s, shift_s, conv_s = [], [], [], [], [], []
    for l in range(depth):
        xp = _ffn(xp, *f1, l)
        *seqs, hs = _mix_in(xp, zeros_sp, p, l, shift=1, seq_rows=tp)
        seqs = [t.reshape(bp, tp, d) for t in seqs]
        oa, s_new = _wkv(*seqs, zeros_s0, p, l, chunk=chunk_p, gb=2 if bp % 2 == 0 else 1)
        xp, nbuf = _mix_out(xp, oa.reshape(bp * tp, d), zeros_hist, p, l, shift=1, seq_rows=tp)
        xp = _ffn(xp, *f2, l)
        wkv_p.append(s_new)
        shift_p.append(hs[:, -1, :])
        conv_p.append(nbuf[:, hist_rows_p - (taps - 1):, :])
        xs = _ffn(xs, *f1, l)
        *seqs, hs = _mix_in(xs, state_shift[l], p, l, shift=bs, seq_rows=ts * bs)
        seqs = [to_bm(t) for t in seqs]
        oa, s_new = _wkv(*seqs, state_wkv[l], p, l, chunk=ts_pad, gb=2 if bs % 2 == 0 else 1)
        oa = jnp.swapaxes(oa[:, :ts, :], 0, 1).reshape(ts * bs, d)
        hist = jnp.swapaxes(state_conv[l], 0, 1).reshape(1, (taps - 1) * bs, dc)
        xs, nbuf = _mix_out(xs, oa, hist, p, l, shift=bs, seq_rows=ts * bs)
        xs = _ffn(xs, *f2, l)
        wkv_s.append(s_new)
        shift_s.append(hs[0])
        conv_s.append(jnp.swapaxes(nbuf.reshape(taps - 1, bs, dc), 0, 1))

    gfin = final_norm[None, :]
    y_prompt = _final_norm(xp, gfin).reshape(bp, tp, d)
    y_sample = jnp.swapaxes(_final_norm(xs, gfin).reshape(ts, bs, d), 0, 1)
    return (y_prompt, y_sample, jnp.stack(wkv_p), jnp.stack(shift_p), jnp.stack(conv_p),
            jnp.stack(wkv_s), jnp.stack(shift_s), jnp.stack(conv_s))
```

```python
import functools
import math

import jax
import jax.numpy as jnp
from jax import lax
from jax.experimental import pallas as pl
from jax.experimental.pallas import tpu as pltpu

F32 = jnp.float32
BF16 = jnp.bfloat16

HEAD = 64
SEG = 256
NORM_EPS = 1e-6
GN_EPS = 64e-5
LN_EPS = 1e-5
VMEM_LIMIT = 56 * 1024 * 1024


def _cparams(sem):
    return pltpu.CompilerParams(dimension_semantics=sem, vmem_limit_bytes=VMEM_LIMIT)


def _dot(a, b):
    return jnp.dot(a.astype(BF16), b.astype(BF16), preferred_element_type=F32)


def _dot_nt(a, b):
    return lax.dot_general(a.astype(BF16), b.astype(BF16), (((1,), (1,)), ((), ())),
                           preferred_element_type=F32)


def _dot_tn(a, b):
    return lax.dot_general(a.astype(BF16), b.astype(BF16), (((0,), (0,)), ((), ())),
                           preferred_element_type=F32)


def _split3(x):
    hi = x.astype(BF16)
    r1 = x - hi.astype(F32)
    mid = r1.astype(BF16)
    lo = (r1 - mid.astype(F32)).astype(BF16)
    return hi, mid, lo


def _dot_exact_rhs(x, m):
    hi, mid, lo = _split3(x)
    d = lambda t: jnp.dot(t, m, preferred_element_type=F32)
    return d(hi) + d(mid) + d(lo)


def _dot_exact_lhs(m, x):
    hi, mid, lo = _split3(x)
    d = lambda t: jnp.dot(m, t, preferred_element_type=F32)
    return d(hi) + d(mid) + d(lo)


def _segsum(x, bo):
    d = x.shape[1]
    parts = [_dot_exact_rhs(x[:, s:s + SEG], bo) for s in range(0, d, SEG)]
    return parts[0] if len(parts) == 1 else jnp.concatenate(parts, axis=1)


def _rmsnorm(x, g):
    return x * lax.rsqrt(jnp.mean(x * x, axis=-1, keepdims=True) + NORM_EPS) * g


def _sigmoid(x):
    return jax.nn.sigmoid(x)


def _pick_tile(n, target, mult=8):
    if n <= target:
        return n
    for t in range(target - target % mult, 0, -mult):
        if n % t == 0:
            return t
    return n


def _ffn_body(x_ref, g_ref, wg_ref, wu_ref, wd_ref, o_ref, hb_ref, acc_ref):
    j = pl.program_id(1)

    @pl.when(j == 0)
    def _():
        hb_ref[...] = _rmsnorm(x_ref[...], g_ref[...]).astype(BF16)
        acc_ref[...] = jnp.zeros_like(acc_ref)

    hb = hb_ref[...]
    gt = jnp.dot(hb, wg_ref[...], preferred_element_type=F32)
    ut = jnp.dot(hb, wu_ref[...], preferred_element_type=F32)
    act = (gt * _sigmoid(gt)) * ut
    acc_ref[...] += jnp.dot(act.astype(BF16), wd_ref[...], preferred_element_type=F32)

    @pl.when(j == pl.num_programs(1) - 1)
    def _():
        o_ref[...] = x_ref[...] + 0.5 * acc_ref[...]


def _ffn(x, g, wg, wu, wd, layer):
    m, d = x.shape
    f = wg.shape[-1]
    tm = _pick_tile(m, 1024)
    tf = _pick_tile(f, 256, 128)
    return pl.pallas_call(
        _ffn_body,
        grid=(m // tm, f // tf),
        in_specs=[
            pl.BlockSpec((tm, d), lambda i, j: (i, 0)),
            pl.BlockSpec((None, 1, d), lambda i, j: (layer, 0, 0)),
            pl.BlockSpec((None, d, tf), lambda i, j: (layer, 0, j)),
            pl.BlockSpec((None, d, tf), lambda i, j: (layer, 0, j)),
            pl.BlockSpec((None, tf, d), lambda i, j: (layer, j, 0)),
        ],
        out_specs=pl.BlockSpec((tm, d), lambda i, j: (i, 0)),
        out_shape=jax.ShapeDtypeStruct((m, d), F32),
        scratch_shapes=[pltpu.VMEM((tm, d), BF16), pltpu.VMEM((tm, d), F32)],
        compiler_params=_cparams(("parallel", "arbitrary")),
        name="ffn",
    )(x, g, wg, wu, wd)


def _mixin_body(x_ref, sp_ref, ng_ref, wa_ref, mu_ref, w0_ref, a0_ref, kk_ref, ka_ref,
                wdec_ref, wiclr_ref, wgate_ref, bo_ref,
                r_ref, lw_ref, kp_ref, v_ref, nkk_ref, b_ref, g_ref, hs_ref,
                carry_ref, *, shift, tiles_per_seq, d, ld, li):
    i = pl.program_id(0)
    tm = x_ref.shape[0]
    h = _rmsnorm(x_ref[...], ng_ref[...])
    hs_ref[...] = h[tm - hs_ref.shape[0]:, :]
    ys = jnp.dot(h.astype(BF16), wa_ref[...], preferred_element_type=F32)

    if shift == 1:
        @pl.when(i % tiles_per_seq == 0)
        def _():
            carry_ref[...] = jnp.zeros_like(carry_ref)

        rows = lax.broadcasted_iota(jnp.int32, ys.shape, 0)
        prev = jnp.where(rows == 0, carry_ref[7:8, :], pltpu.roll(ys, 1, axis=0))
        carry_ref[...] = ys[tm - 8:, :]
    else:
        yp = jnp.dot(sp_ref[...].astype(BF16), wa_ref[...], preferred_element_type=F32)
        prev = jnp.concatenate([yp, ys[:tm - shift, :]], axis=0)

    xs = ys + (prev - ys) * mu_ref[...]
    r = xs[:, :d]
    k = xs[:, d:2 * d]
    v = xs[:, 2 * d:3 * d]
    o2 = 3 * d
    xw = xs[:, o2:o2 + ld]
    xa = xs[:, o2 + ld:o2 + ld + li]
    xg = xs[:, o2 + ld + li:]

    z = -(w0_ref[...] + _dot(jnp.tanh(xw), wdec_ref[...]))
    softplus = jnp.maximum(z, 0.0) + jnp.log(1.0 + jnp.exp(-jnp.abs(z)))
    w_log = -softplus - 0.5
    lw = -jnp.exp(w_log)
    a = _sigmoid(a0_ref[...] + _dot(xa, wiclr_ref[...]))
    g = _dot(_sigmoid(xg), wgate_ref[...])

    kk = k * kk_ref[...]
    nrm = jnp.sqrt(_segsum(kk * kk, bo_ref[...]))
    kk = kk / jnp.maximum(nrm, 1e-12)
    kp = k * (1.0 + (a - 1.0) * ka_ref[...])

    r_ref[...] = r
    lw_ref[...] = lw
    kp_ref[...] = kp
    v_ref[...] = v
    nkk_ref[...] = -kk
    b_ref[...] = kk * a
    g_ref[...] = g


def _mix_in(x, sp, p, layer, *, shift, seq_rows):
    m, d = x.shape
    sw = p['wa'].shape[-1]
    ld = p['wdec'].shape[1]
    li = p['wiclr'].shape[1]
    lg = p['wgate'].shape[1]
    if shift == 1:
        tm = _pick_tile(seq_rows, 256)
        tps = seq_rows // tm
        hs_rows = 8
    else:
        tm = m
        tps = 1
        hs_rows = shift
    nt = m // tm
    vec = lambda: pl.BlockSpec((None, 1, d), lambda i: (layer, 0, 0))
    tile = lambda: pl.BlockSpec((tm, d), lambda i: (i, 0))
    body = functools.partial(_mixin_body, shift=shift, tiles_per_seq=tps, d=d, ld=ld, li=li)
    outs = pl.pallas_call(
        body,
        grid=(nt,),
        in_specs=[
            tile(),
            pl.BlockSpec(sp.shape, lambda i: (0, 0)),
            vec(),
            pl.BlockSpec((None, d, sw), lambda i: (layer, 0, 0)),
            pl.BlockSpec((None, 1, sw), lambda i: (layer, 0, 0)),
            vec(), vec(), vec(), vec(),
            pl.BlockSpec((None, ld, d), lambda i: (layer, 0, 0)),
            pl.BlockSpec((None, li, d), lambda i: (layer, 0, 0)),
            pl.BlockSpec((None, lg, d), lambda i: (layer, 0, 0)),
            pl.BlockSpec((SEG, SEG), lambda i: (0, 0)),
        ],
        out_specs=[tile() for _ in range(7)] + [pl.BlockSpec((None, hs_rows, d), lambda i: (i // tps, 0, 0))],
        out_shape=[jax.ShapeDtypeStruct((m, d), F32) for _ in range(7)]
                  + [jax.ShapeDtypeStruct((nt // tps, hs_rows, d), F32)],
        scratch_shapes=[pltpu.VMEM((8, sw), F32)],
        compiler_params=_cparams(("arbitrary",)),
        name="mix_in",
    )(x, sp, p['mix_norm'], p['wa'], p['mu'], p['w0'], p['a0'], p['k_k'], p['k_a'],
      p['wdec'], p['wiclr'], p['wgate'], p['bo'])
    return outs


def _wkv_body(r_ref, lw_ref, kp_ref, v_ref, a_ref, b_ref, g_ref, s0_ref,
              lng_ref, lnb_ref, rk_ref, bo_ref, tri_ref,
              oa_ref, sout_ref, s_ref, y_ref, *, heads):
    c = pl.program_id(1)
    gb, ch, d = r_ref.shape
    pairs = heads // 2
    pw2 = 2 * HEAD

    @pl.when(c == 0)
    def _():
        for i in range(gb):
            for p in range(pairs):
                s_ref[i, p] = jnp.concatenate([s0_ref[i, 2 * p], s0_ref[i, 2 * p + 1]], axis=1)

    tri = tri_ref[...]
    left_v = lax.broadcasted_iota(jnp.int32, (1, pw2), 1) < HEAD
    rows = lax.broadcasted_iota(jnp.int32, (ch, 2 * ch), 0)
    cols = lax.broadcasted_iota(jnp.int32, (ch, 2 * ch), 1)
    left_c = cols < ch
    cin = jnp.where(left_c, cols, cols - ch)
    strict = cin < rows
    incl = cin <= rows
    eye = (cin == rows).astype(F32)
    bo = bo_ref[...]

    def bdiag(x, left):
        return jnp.concatenate([jnp.where(left, x, 0.0), jnp.where(left, 0.0, x)], axis=0).astype(BF16)

    seqs = []
    for i in range(gb):
        lw = lw_ref[i]
        cum = _dot_exact_lhs(tri, lw)
        p_in = jnp.exp(cum)
        p_inv = jnp.exp(-cum)
        seqs.append(dict(at=a_ref[i] * jnp.exp(cum - lw), rt=r_ref[i] * p_in, bt=b_ref[i] * p_inv,
                         kt=kp_ref[i] * p_inv, v=v_ref[i], pend=p_in[ch - 1:ch, :]))

    units = [(i, p) for i in range(gb) for p in range(pairs)]
    ur = range(len(units))
    sls = [slice(p * pw2, (p + 1) * pw2) for _, p in units]
    cut = lambda name: [seqs[i][name][:, sl] for (i, _), sl in zip(units, sls)]
    atp, rtp, btp, ktp, vp = cut('at'), cut('rt'), cut('bt'), cut('kt'), cut('v')
    s_old = [s_ref[i, p] for i, p in units]
    ar = [jnp.concatenate([atp[n], rtp[n]], axis=0).astype(BF16) for n in ur]
    mb = [_dot_nt(ar[n], bdiag(btp[n], left_v)) for n in ur]
    mk = [_dot_nt(ar[n], bdiag(ktp[n], left_v)) for n in ur]
    ars = [_dot_nt(ar[n], bdiag(s_old[n], left_v)) for n in ur]
    lab = [jnp.where(strict, m[:ch], 0.0) for m in mb]
    lak = [jnp.where(strict, m[:ch], 0.0) for m in mk]
    mrb = [jnp.where(incl, m[ch:], 0.0) for m in mb]
    mrk = [jnp.where(incl, m[ch:], 0.0) for m in mk]
    tinv = [eye + l for l in lab]
    pw = [_dot(m, bdiag(m, left_c)) for m in lab]
    for _ in range(int(math.log2(ch)) - 2):
        stk = [_dot(jnp.concatenate([t, m], axis=0), bdiag(m, left_c)) for t, m in zip(tinv, pw)]
        tinv = [t + s[:ch] for t, s in zip(tinv, stk)]
        pw = [s[ch:] for s in stk]
    tinv = [t + _dot(t, bdiag(m, left_c)) for t, m in zip(tinv, pw)]
    kv = [_dot(jnp.concatenate([lak[n], mrk[n]], axis=0), bdiag(vp[n], left_v)) for n in ur]
    rhs = [ars[n][:ch] + kv[n][:ch] for n in ur]
    u = [_dot(tinv[n], bdiag(rhs[n], left_v)) for n in ur]
    ys = [ars[n][ch:] + _dot(mrb[n], bdiag(u[n], left_v)) + kv[n][ch:] for n in ur]
    full = [_dot_tn(jnp.concatenate([u[n], vp[n]], axis=0),
                    jnp.concatenate([btp[n], ktp[n]], axis=0)) for n in ur]
    for n, (i, p) in enumerate(units):
        upd = jnp.where(left_v, full[n][:HEAD], full[n][HEAD:])
        y_ref[i, :, sls[n]] = ys[n]
        s_ref[i, p] = (s_old[n] + upd) * seqs[i]['pend'][:, sls[n]]

    for i in range(gb):
        y = y_ref[i]
        mean = _segsum(y, bo) * (1.0 / HEAD)
        dlt = y - mean
        var = _segsum(dlt * dlt, bo) * (1.0 / HEAD)
        gn = dlt * lax.rsqrt(var + GN_EPS) * lng_ref[...] + lnb_ref[...]
        bonus = _segsum(r_ref[i] * kp_ref[i] * rk_ref[...], bo) * v_ref[i]
        oa_ref[i] = (gn + bonus) * g_ref[i]

    @pl.when(c == pl.num_programs(1) - 1)
    def _():
        for i in range(gb):
            for p in range(pairs):
                sp = s_ref[i, p]
                sout_ref[i, 2 * p] = sp[:, :HEAD]
                sout_ref[i, 2 * p + 1] = sp[:, HEAD:]


def _wkv(r, lw, kp, v, a, b, g, s0, p, layer, *, chunk, gb):
    nb, t, d = r.shape
    heads = d // HEAD
    seq = lambda: pl.BlockSpec((gb, chunk, d), lambda bi, ci: (bi, ci, 0))
    st = lambda: pl.BlockSpec((gb, heads, HEAD, HEAD), lambda bi, ci: (bi, 0, 0, 0))
    vec = lambda: pl.BlockSpec((None, 1, d), lambda bi, ci: (layer, 0, 0))
    tri = jnp.tril(jnp.ones((chunk, chunk), F32)).astype(BF16)
    return pl.pallas_call(
        functools.partial(_wkv_body, heads=heads),
        grid=(nb // gb, t // chunk),
        in_specs=[seq() for _ in range(7)] + [st(), vec(), vec(), vec(),
                  pl.BlockSpec((SEG, SEG), lambda bi, ci: (0, 0)),
                  pl.BlockSpec((chunk, chunk), lambda bi, ci: (0, 0))],
        out_specs=[seq(), st()],
        out_shape=[jax.ShapeDtypeStruct((nb, t, d), F32),
                   jax.ShapeDtypeStruct((nb, heads, HEAD, HEAD), F32)],
        scratch_shapes=[pltpu.VMEM((gb, heads // 2, HEAD, 2 * HEAD), F32), pltpu.VMEM((gb, chunk, d), F32)],
        compiler_params=_cparams(("parallel", "arbitrary")),
        name="wkv",
    )(r, lw, kp, v, a, b, g, s0, p['ln_x_g'], p['ln_x_b'], p['r_k'], p['bo'], tri)


def _mixout_body(x_ref, oa_ref, hist_ref, ng_ref, wb_ref, cw_ref, cb_ref, clg_ref, clb_ref,
                 wco_ref, wo_ref, o_ref, nb_ref, up_ref, win_ref, *, shift, tiles_per_seq, d, dc, taps):
    i = pl.program_id(0)
    tm = x_ref.shape[0]
    hist = hist_ref.shape[0]
    off = hist - (taps - 1) * shift

    @pl.when(i % tiles_per_seq == 0)
    def _():
        up_ref[0:hist, :] = hist_ref[...]

    x = x_ref[...]
    hb = _rmsnorm(x, ng_ref[...]).astype(BF16)
    cg = jnp.dot(hb, wb_ref[:, :2 * dc], preferred_element_type=F32)
    u = cg[:, :dc] * _sigmoid(cg[:, dc:])
    up_ref[hist:hist + tm, :] = u
    dw = jnp.zeros((tm, dc), F32) + cb_ref[...]
    if shift % 8 == 0:
        for tau in range(taps):
            dw = dw + up_ref[off + tau * shift:off + tau * shift + tm, :] * cw_ref[tau:tau + 1, :]
    else:
        for phase in range(8):
            taus = [t for t in range(taps) if (off + t * shift) % 8 == phase]
            if not taus:
                continue
            span = max(off + t * shift - phase for t in taus) + tm
            win_ref[0:span, :] = up_ref[phase:phase + span, :]
            for t in taus:
                q = off + t * shift - phase
                dw = dw + win_ref[q:q + tm, :] * cw_ref[t:t + 1, :]
    nb_ref[...] = up_ref[tm:tm + hist, :]
    if tiles_per_seq > 1:
        up_ref[0:hist, :] = up_ref[tm:tm + hist, :]

    mu = jnp.mean(dw, axis=-1, keepdims=True)
    dd = dw - mu
    var = jnp.mean(dd * dd, axis=-1, keepdims=True)
    ln = dd * lax.rsqrt(var + LN_EPS) * clg_ref[...] + clb_ref[...]
    ob = _dot(ln * _sigmoid(ln), wco_ref[...])
    ga = _sigmoid(jnp.dot(hb, wb_ref[:, 2 * dc:2 * dc + d], preferred_element_type=F32))
    gb = _sigmoid(jnp.dot(hb, wb_ref[:, 2 * dc + d:], preferred_element_type=F32))
    z = ga * oa_ref[...] + gb * ob
    o_ref[...] = x + _dot(z, wo_ref[...])


def _mix_out(x, oa, hist, p, layer, *, shift, seq_rows):
    m, d = x.shape
    dc = p['wco'].shape[1]
    taps = p['conv_w'].shape[1]
    wbw = p['wb'].shape[-1]
    hrows = hist.shape[1]
    if shift == 1:
        tm = _pick_tile(seq_rows, 256)
        tps = seq_rows // tm
    else:
        tm = m
        tps = 1
    nt = m // tm
    assert tps == 1 or tm >= hrows
    tile = lambda: pl.BlockSpec((tm, d), lambda i: (i, 0))
    body = functools.partial(_mixout_body, shift=shift, tiles_per_seq=tps, d=d, dc=dc, taps=taps)
    return pl.pallas_call(
        body,
        grid=(nt,),
        in_specs=[
            tile(), tile(),
            pl.BlockSpec((None, hrows, dc), lambda i: (i // tps, 0, 0)),
            pl.BlockSpec((None, 1, d), lambda i: (layer, 0, 0)),
            pl.BlockSpec((None, d, wbw), lambda i: (layer, 0, 0)),
            pl.BlockSpec((None, taps, dc), lambda i: (layer, 0, 0)),
            pl.BlockSpec((None, 1, dc), lambda i: (layer, 0, 0)),
            pl.BlockSpec((None, 1, dc), lambda i: (layer, 0, 0)),
            pl.BlockSpec((None, 1, dc), lambda i: (layer, 0, 0)),
            pl.BlockSpec((None, dc, d), lambda i: (layer, 0, 0)),
            pl.BlockSpec((None, d, d), lambda i: (layer, 0, 0)),
        ],
        out_specs=[tile(), pl.BlockSpec((None, hrows, dc), lambda i: (i // tps, 0, 0))],
        out_shape=[jax.ShapeDtypeStruct((m, d), F32),
                   jax.ShapeDtypeStruct((nt // tps, hrows, dc), F32)],
        scratch_shapes=[pltpu.VMEM((hrows + tm, dc), F32),
                        pltpu.VMEM((hrows + tm if shift % 8 else 8, dc), F32)],
        compiler_params=_cparams(("arbitrary",)),
        name="mix_out",
    )(x, oa, hist, p['mix_norm'], p['wb'], p['conv_w'], p['conv_b'], p['conv_ln_g'],
      p['conv_ln_b'], p['wco'], p['wo'])


def _norm_body(x_ref, g_ref, o_ref):
    o_ref[...] = _rmsnorm(x_ref[...], g_ref[...])


def _final_norm(x, g):
    m, d = x.shape
    tm = _pick_tile(m, 1024)
    return pl.pallas_call(
        _norm_body,
        grid=(m // tm,),
        in_specs=[pl.BlockSpec((tm, d), lambda i: (i, 0)), pl.BlockSpec((1, d), lambda i: (0, 0))],
        out_specs=pl.BlockSpec((tm, d), lambda i: (i, 0)),
        out_shape=jax.ShapeDtypeStruct((m, d), F32),
        compiler_params=_cparams(("parallel",)),
        name="final_norm",
    )(x, g)


def kernel(x_prompt, x_sample, state_wkv, state_shift, state_conv, ffn1_norm, ffn1_w_gate, ffn1_w_up, ffn1_w_down, mix_norm, w_in, mu, w0, w_decay_up, a0, w_iclr_up, w_gate_up, k_k, k_a, r_k, ln_x_g, ln_x_b, conv_w, conv_b, conv_ln_g, conv_ln_b, w_conv_out, w_out, ffn2_norm, ffn2_w_gate, ffn2_w_up, ffn2_w_down, final_norm):
    bp, tp, d = x_prompt.shape
    bs, ts, _ = x_sample.shape
    depth = w_in.shape[0]
    heads = d // HEAD
    dc = conv_w.shape[-1]
    taps = conv_w.shape[1]
    ld, li, lg = w_decay_up.shape[1], w_iclr_up.shape[1], w_gate_up.shape[1]
    sw = 3 * d + ld + li + lg
    row = lambda t: t[:, None, :]
    seg = jnp.arange(SEG) // HEAD
    p = {
        'mix_norm': row(mix_norm), 'mu': row(mu), 'w0': row(w0), 'a0': row(a0), 'k_k': row(k_k),
        'k_a': row(k_a), 'r_k': r_k.reshape(depth, 1, d), 'ln_x_g': row(ln_x_g), 'ln_x_b': row(ln_x_b),
        'conv_w': conv_w, 'conv_b': row(conv_b), 'conv_ln_g': row(conv_ln_g), 'conv_ln_b': row(conv_ln_b),
        'wa': w_in[:, :, :sw].astype(BF16), 'wb': w_in[:, :, sw:].astype(BF16),
        'wdec': w_decay_up.astype(BF16), 'wiclr': w_iclr_up.astype(BF16), 'wgate': w_gate_up.astype(BF16),
        'wco': w_conv_out.astype(BF16), 'wo': w_out.astype(BF16),
        'bo': (seg[:, None] == seg[None, :]).astype(BF16),
    }
    f1 = (row(ffn1_norm), ffn1_w_gate.astype(BF16), ffn1_w_up.astype(BF16), ffn1_w_down.astype(BF16))
    f2 = (row(ffn2_norm), ffn2_w_gate.astype(BF16), ffn2_w_up.astype(BF16), ffn2_w_down.astype(BF16))

    xp = x_prompt.reshape(bp * tp, d)
    xs = jnp.swapaxes(x_sample, 0, 1).reshape(ts * bs, d)
    hist_rows_p = -(-(taps - 1) // 8) * 8
    zeros_sp = jnp.zeros((8, d), F32)
    zeros_hist = jnp.zeros((bp, hist_rows_p, dc), F32)
    zeros_s0 = jnp.zeros((bp, heads, HEAD, HEAD), F32)
    chunk_p = _pick_tile(tp, 64)
    ts_pad = -(-ts // 8) * 8

    def to_bm(t):
        t = jnp.swapaxes(t.reshape(ts, bs, d), 0, 1)
        return jnp.pad(t, ((0, 0), (0, ts_pad - ts), (0, 0)))

    wkv_p, shift_p, conv_p, wkv_s, shift_s, conv_s = [], [], [], [], [], []
    for l in range(depth):
        xp = _ffn(xp, *f1, l)
        *seqs, hs = _mix_in(xp, zeros_sp, p, l, shift=1, seq_rows=tp)
        seqs = [t.reshape(bp, tp, d) for t in seqs]
        oa, s_new = _wkv(*seqs, zeros_s0, p, l, chunk=chunk_p, gb=4 if bp % 4 == 0 else 1)
        xp, nbuf = _mix_out(xp, oa.reshape(bp * tp, d), zeros_hist, p, l, shift=1, seq_rows=tp)
        xp = _ffn(xp, *f2, l)
        wkv_p.append(s_new)
        shift_p.append(hs[:, -1, :])
        conv_p.append(nbuf[:, hist_rows_p - (taps - 1):, :])
        xs = _ffn(xs, *f1, l)
        *seqs, hs = _mix_in(xs, state_shift[l], p, l, shift=bs, seq_rows=ts * bs)
        seqs = [to_bm(t) for t in seqs]
        oa, s_new = _wkv(*seqs, state_wkv[l], p, l, chunk=ts_pad, gb=2 if bs % 2 == 0 else 1)
        oa = jnp.swapaxes(oa[:, :ts, :], 0, 1).reshape(ts * bs, d)
        hist = jnp.swapaxes(state_conv[l], 0, 1).reshape(1, (taps - 1) * bs, dc)
        xs, nbuf = _mix_out(xs, oa, hist, p, l, shift=bs, seq_rows=ts * bs)
        xs = _ffn(xs, *f2, l)
        wkv_s.append(s_new)
        shift_s.append(hs[0])
        conv_s.append(jnp.swapaxes(nbuf.reshape(taps - 1, bs, dc), 0, 1))

    gfin = final_norm[None, :]
    y_prompt = _final_norm(xp, gfin).reshape(bp, tp, d)
    y_sample = jnp.swapaxes(_final_norm(xs, gfin).reshape(ts, bs, d), 0, 1)
    return (y_prompt, y_sample, jnp.stack(wkv_p), jnp.stack(shift_p), jnp.stack(conv_p),
            jnp.stack(wkv_s), jnp.stack(shift_s), jnp.stack(conv_s))
```

```python
import functools
import math

import jax
import jax.numpy as jnp
from jax import lax
from jax.experimental import pallas as pl
from jax.experimental.pallas import tpu as pltpu

F32 = jnp.float32
BF16 = jnp.bfloat16

HEAD = 64
SEG = 256
NORM_EPS = 1e-6
GN_EPS = 64e-5
LN_EPS = 1e-5
VMEM_LIMIT = 56 * 1024 * 1024


def _cparams(sem):
    return pltpu.CompilerParams(dimension_semantics=sem, vmem_limit_bytes=VMEM_LIMIT)


def _dot(a, b):
    return jnp.dot(a.astype(BF16), b.astype(BF16), preferred_element_type=F32)


def _dot_nt(a, b):
    return lax.dot_general(a.astype(BF16), b.astype(BF16), (((1,), (1,)), ((), ())),
                           preferred_element_type=F32)


def _dot_tn(a, b):
    return lax.dot_general(a.astype(BF16), b.astype(BF16), (((0,), (0,)), ((), ())),
                           preferred_element_type=F32)


def _split3(x):
    hi = x.astype(BF16)
    r1 = x - hi.astype(F32)
    mid = r1.astype(BF16)
    lo = (r1 - mid.astype(F32)).astype(BF16)
    return hi, mid, lo


def _dot_exact_rhs(x, m):
    hi, mid, lo = _split3(x)
    d = lambda t: jnp.dot(t, m, preferred_element_type=F32)
    return d(hi) + d(mid) + d(lo)


def _dot_exact_lhs(m, x):
    hi, mid, lo = _split3(x)
    d = lambda t: jnp.dot(m, t, preferred_element_type=F32)
    return d(hi) + d(mid) + d(lo)


def _segsum(x, bo):
    d = x.shape[1]
    parts = [_dot_exact_rhs(x[:, s:s + SEG], bo) for s in range(0, d, SEG)]
    return parts[0] if len(parts) == 1 else jnp.concatenate(parts, axis=1)


def _rmsnorm(x, g):
    return x * lax.rsqrt(jnp.mean(x * x, axis=-1, keepdims=True) + NORM_EPS) * g


def _sigmoid(x):
    return jax.nn.sigmoid(x)


def _pick_tile(n, target, mult=8):
    if n <= target:
        return n
    for t in range(target - target % mult, 0, -mult):
        if n % t == 0:
            return t
    return n


def _ffn_body(x_ref, g_ref, wg_ref, wu_ref, wd_ref, o_ref, hb_ref, acc_ref):
    j = pl.program_id(1)

    @pl.when(j == 0)
    def _():
        hb_ref[...] = _rmsnorm(x_ref[...], g_ref[...]).astype(BF16)
        acc_ref[...] = jnp.zeros_like(acc_ref)

    hb = hb_ref[...]
    gt = jnp.dot(hb, wg_ref[...], preferred_element_type=F32)
    ut = jnp.dot(hb, wu_ref[...], preferred_element_type=F32)
    act = (gt * _sigmoid(gt)) * ut
    acc_ref[...] += jnp.dot(act.astype(BF16), wd_ref[...], preferred_element_type=F32)

    @pl.when(j == pl.num_programs(1) - 1)
    def _():
        o_ref[...] = x_ref[...] + 0.5 * acc_ref[...]


def _ffn(x, g, wg, wu, wd, layer):
    m, d = x.shape
    f = wg.shape[-1]
    tm = _pick_tile(m, 2048)
    tf = _pick_tile(f, 256, 128)
    return pl.pallas_call(
        _ffn_body,
        grid=(m // tm, f // tf),
        in_specs=[
            pl.BlockSpec((tm, d), lambda i, j: (i, 0)),
            pl.BlockSpec((None, 1, d), lambda i, j: (layer, 0, 0)),
            pl.BlockSpec((None, d, tf), lambda i, j: (layer, 0, j)),
            pl.BlockSpec((None, d, tf), lambda i, j: (layer, 0, j)),
            pl.BlockSpec((None, tf, d), lambda i, j: (layer, j, 0)),
        ],
        out_specs=pl.BlockSpec((tm, d), lambda i, j: (i, 0)),
        out_shape=jax.ShapeDtypeStruct((m, d), F32),
        scratch_shapes=[pltpu.VMEM((tm, d), BF16), pltpu.VMEM((tm, d), F32)],
        compiler_params=_cparams(("parallel", "arbitrary")),
        name="ffn",
    )(x, g, wg, wu, wd)


def _mixin_body(x_ref, sp_ref, ng_ref, wa_ref, mu_ref, w0_ref, a0_ref, kk_ref, ka_ref,
                wdec_ref, wiclr_ref, wgate_ref, bo_ref,
                r_ref, lw_ref, kp_ref, v_ref, nkk_ref, b_ref, g_ref, hs_ref,
                carry_ref, *, shift, tiles_per_seq, d, ld, li):
    i = pl.program_id(0)
    tm = x_ref.shape[0]
    h = _rmsnorm(x_ref[...], ng_ref[...])
    hs_ref[...] = h[tm - hs_ref.shape[0]:, :]
    ys = jnp.dot(h.astype(BF16), wa_ref[...], preferred_element_type=F32)

    if shift == 1:
        @pl.when(i % tiles_per_seq == 0)
        def _():
            carry_ref[...] = jnp.zeros_like(carry_ref)

        rows = lax.broadcasted_iota(jnp.int32, ys.shape, 0)
        prev = jnp.where(rows == 0, carry_ref[7:8, :], pltpu.roll(ys, 1, axis=0))
        carry_ref[...] = ys[tm - 8:, :]
    else:
        yp = jnp.dot(sp_ref[...].astype(BF16), wa_ref[...], preferred_element_type=F32)
        prev = jnp.concatenate([yp, ys[:tm - shift, :]], axis=0)

    xs = ys + (prev - ys) * mu_ref[...]
    r = xs[:, :d]
    k = xs[:, d:2 * d]
    v = xs[:, 2 * d:3 * d]
    o2 = 3 * d
    xw = xs[:, o2:o2 + ld]
    xa = xs[:, o2 + ld:o2 + ld + li]
    xg = xs[:, o2 + ld + li:]

    z = -(w0_ref[...] + _dot(jnp.tanh(xw), wdec_ref[...]))
    softplus = jnp.maximum(z, 0.0) + jnp.log(1.0 + jnp.exp(-jnp.abs(z)))
    w_log = -softplus - 0.5
    lw = -jnp.exp(w_log)
    a = _sigmoid(a0_ref[...] + _dot(xa, wiclr_ref[...]))
    g = _dot(_sigmoid(xg), wgate_ref[...])

    kk = k * kk_ref[...]
    nrm = jnp.sqrt(_segsum(kk * kk, bo_ref[...]))
    kk = kk / jnp.maximum(nrm, 1e-12)
    kp = k * (1.0 + (a - 1.0) * ka_ref[...])

    r_ref[...] = r
    lw_ref[...] = lw
    kp_ref[...] = kp
    v_ref[...] = v
    nkk_ref[...] = -kk
    b_ref[...] = kk * a
    g_ref[...] = g


def _mix_in(x, sp, p, layer, *, shift, seq_rows):
    m, d = x.shape
    sw = p['wa'].shape[-1]
    ld = p['wdec'].shape[1]
    li = p['wiclr'].shape[1]
    lg = p['wgate'].shape[1]
    if shift == 1:
        tm = _pick_tile(seq_rows, 256)
        tps = seq_rows // tm
        hs_rows = 8
    else:
        tm = m
        tps = 1
        hs_rows = shift
    nt = m // tm
    vec = lambda: pl.BlockSpec((None, 1, d), lambda i: (layer, 0, 0))
    tile = lambda: pl.BlockSpec((tm, d), lambda i: (i, 0))
    body = functools.partial(_mixin_body, shift=shift, tiles_per_seq=tps, d=d, ld=ld, li=li)
    outs = pl.pallas_call(
        body,
        grid=(nt,),
        in_specs=[
            tile(),
            pl.BlockSpec(sp.shape, lambda i: (0, 0)),
            vec(),
            pl.BlockSpec((None, d, sw), lambda i: (layer, 0, 0)),
            pl.BlockSpec((None, 1, sw), lambda i: (layer, 0, 0)),
            vec(), vec(), vec(), vec(),
            pl.BlockSpec((None, ld, d), lambda i: (layer, 0, 0)),
            pl.BlockSpec((None, li, d), lambda i: (layer, 0, 0)),
            pl.BlockSpec((None, lg, d), lambda i: (layer, 0, 0)),
            pl.BlockSpec((SEG, SEG), lambda i: (0, 0)),
        ],
        out_specs=[tile() for _ in range(7)] + [pl.BlockSpec((None, hs_rows, d), lambda i: (i // tps, 0, 0))],
        out_shape=[jax.ShapeDtypeStruct((m, d), F32) for _ in range(7)]
                  + [jax.ShapeDtypeStruct((nt // tps, hs_rows, d), F32)],
        scratch_shapes=[pltpu.VMEM((8, sw), F32)],
        compiler_params=_cparams(("arbitrary",)),
        name="mix_in",
    )(x, sp, p['mix_norm'], p['wa'], p['mu'], p['w0'], p['a0'], p['k_k'], p['k_a'],
      p['wdec'], p['wiclr'], p['wgate'], p['bo'])
    return outs


def _wkv_body(r_ref, lw_ref, kp_ref, v_ref, a_ref, b_ref, g_ref, s0_ref,
              lng_ref, lnb_ref, rk_ref, bo_ref, tri_ref,
              oa_ref, sout_ref, s_ref, y_ref, *, heads):
    c = pl.program_id(1)
    gb, ch, d = r_ref.shape
    pairs = heads // 2
    pw2 = 2 * HEAD

    @pl.when(c == 0)
    def _():
        for i in range(gb):
            for p in range(pairs):
                s_ref[i, p] = jnp.concatenate([s0_ref[i, 2 * p], s0_ref[i, 2 * p + 1]], axis=1)

    tri = tri_ref[...]
    left_v = lax.broadcasted_iota(jnp.int32, (1, pw2), 1) < HEAD
    rows = lax.broadcasted_iota(jnp.int32, (ch, 2 * ch), 0)
    cols = lax.broadcasted_iota(jnp.int32, (ch, 2 * ch), 1)
    left_c = cols < ch
    cin = jnp.where(left_c, cols, cols - ch)
    strict = cin < rows
    incl = cin <= rows
    eye = (cin == rows).astype(F32)
    bo = bo_ref[...]

    def bdiag(x, left):
        return jnp.concatenate([jnp.where(left, x, 0.0), jnp.where(left, 0.0, x)], axis=0).astype(BF16)

    seqs = []
    for i in range(gb):
        lw = lw_ref[i]
        cum = _dot_exact_lhs(tri, lw)
        p_in = jnp.exp(cum)
        p_inv = jnp.exp(-cum)
        seqs.append(dict(at=a_ref[i] * jnp.exp(cum - lw), rt=r_ref[i] * p_in, bt=b_ref[i] * p_inv,
                         kt=kp_ref[i] * p_inv, v=v_ref[i], pend=p_in[ch - 1:ch, :]))

    units = [(i, p) for i in range(gb) for p in range(pairs)]
    ur = range(len(units))
    sls = [slice(p * pw2, (p + 1) * pw2) for _, p in units]
    cut = lambda name: [seqs[i][name][:, sl] for (i, _), sl in zip(units, sls)]
    atp, rtp, btp, ktp, vp = cut('at'), cut('rt'), cut('bt'), cut('kt'), cut('v')
    s_old = [s_ref[i, p] for i, p in units]
    ar = [jnp.concatenate([atp[n], rtp[n]], axis=0).astype(BF16) for n in ur]
    mb = [_dot_nt(ar[n], bdiag(btp[n], left_v)) for n in ur]
    mk = [_dot_nt(ar[n], bdiag(ktp[n], left_v)) for n in ur]
    ars = [_dot_nt(ar[n], bdiag(s_old[n], left_v)) for n in ur]
    lab = [jnp.where(strict, m[:ch], 0.0) for m in mb]
    lak = [jnp.where(strict, m[:ch], 0.0) for m in mk]
    mrb = [jnp.where(incl, m[ch:], 0.0) for m in mb]
    mrk = [jnp.where(incl, m[ch:], 0.0) for m in mk]
    tinv = [eye + l for l in lab]
    pw = [_dot(m, bdiag(m, left_c)) for m in lab]
    for _ in range(int(math.log2(ch)) - 2):
        stk = [_dot(jnp.concatenate([t, m], axis=0), bdiag(m, left_c)) for t, m in zip(tinv, pw)]
        tinv = [t + s[:ch] for t, s in zip(tinv, stk)]
        pw = [s[ch:] for s in stk]
    tinv = [t + _dot(t, bdiag(m, left_c)) for t, m in zip(tinv, pw)]
    kv = [_dot(jnp.concatenate([lak[n], mrk[n]], axis=0), bdiag(vp[n], left_v)) for n in ur]
    rhs = [ars[n][:ch] + kv[n][:ch] for n in ur]
    u = [_dot(tinv[n], bdiag(rhs[n], left_v)) for n in ur]
    ys = [ars[n][ch:] + _dot(mrb[n], bdiag(u[n], left_v)) + kv[n][ch:] for n in ur]
    full = [_dot_tn(jnp.concatenate([u[n], vp[n]], axis=0),
                    jnp.concatenate([btp[n], ktp[n]], axis=0)) for n in ur]
    for n, (i, p) in enumerate(units):
        upd = jnp.where(left_v, full[n][:HEAD], full[n][HEAD:])
        y_ref[i, :, sls[n]] = ys[n]
        s_ref[i, p] = (s_old[n] + upd) * seqs[i]['pend'][:, sls[n]]

    for i in range(gb):
        y = y_ref[i]
        mean = _segsum(y, bo) * (1.0 / HEAD)
        dlt = y - mean
        var = _segsum(dlt * dlt, bo) * (1.0 / HEAD)
        gn = dlt * lax.rsqrt(var + GN_EPS) * lng_ref[...] + lnb_ref[...]
        bonus = _segsum(r_ref[i] * kp_ref[i] * rk_ref[...], bo) * v_ref[i]
        oa_ref[i] = (gn + bonus) * g_ref[i]

    @pl.when(c == pl.num_programs(1) - 1)
    def _():
        for i in range(gb):
            for p in range(pairs):
                sp = s_ref[i, p]
                sout_ref[i, 2 * p] = sp[:, :HEAD]
                sout_ref[i, 2 * p + 1] = sp[:, HEAD:]


def _wkv(r, lw, kp, v, a, b, g, s0, p, layer, *, chunk, gb):
    nb, t, d = r.shape
    heads = d // HEAD
    seq = lambda: pl.BlockSpec((gb, chunk, d), lambda bi, ci: (bi, ci, 0))
    st = lambda: pl.BlockSpec((gb, heads, HEAD, HEAD), lambda bi, ci: (bi, 0, 0, 0))
    vec = lambda: pl.BlockSpec((None, 1, d), lambda bi, ci: (layer, 0, 0))
    tri = jnp.tril(jnp.ones((chunk, chunk), F32)).astype(BF16)
    return pl.pallas_call(
        functools.partial(_wkv_body, heads=heads),
        grid=(nb // gb, t // chunk),
        in_specs=[seq() for _ in range(7)] + [st(), vec(), vec(), vec(),
                  pl.BlockSpec((SEG, SEG), lambda bi, ci: (0, 0)),
                  pl.BlockSpec((chunk, chunk), lambda bi, ci: (0, 0))],
        out_specs=[seq(), st()],
        out_shape=[jax.ShapeDtypeStruct((nb, t, d), F32),
                   jax.ShapeDtypeStruct((nb, heads, HEAD, HEAD), F32)],
        scratch_shapes=[pltpu.VMEM((gb, heads // 2, HEAD, 2 * HEAD), F32), pltpu.VMEM((gb, chunk, d), F32)],
        compiler_params=_cparams(("parallel", "arbitrary")),
        name="wkv",
    )(r, lw, kp, v, a, b, g, s0, p['ln_x_g'], p['ln_x_b'], p['r_k'], p['bo'], tri)


def _mixout_body(x_ref, oa_ref, hist_ref, ng_ref, wb_ref, cw_ref, cb_ref, clg_ref, clb_ref,
                 wco_ref, wo_ref, o_ref, nb_ref, up_ref, win_ref, *, shift, tiles_per_seq, d, dc, taps):
    i = pl.program_id(0)
    tm = x_ref.shape[0]
    hist = hist_ref.shape[0]
    off = hist - (taps - 1) * shift

    @pl.when(i % tiles_per_seq == 0)
    def _():
        up_ref[0:hist, :] = hist_ref[...]

    x = x_ref[...]
    hb = _rmsnorm(x, ng_ref[...]).astype(BF16)
    cg = jnp.dot(hb, wb_ref[:, :2 * dc], preferred_element_type=F32)
    u = cg[:, :dc] * _sigmoid(cg[:, dc:])
    up_ref[hist:hist + tm, :] = u
    dw = jnp.zeros((tm, dc), F32) + cb_ref[...]
    if shift % 8 == 0:
        for tau in range(taps):
            dw = dw + up_ref[off + tau * shift:off + tau * shift + tm, :] * cw_ref[tau:tau + 1, :]
    else:
        for phase in range(8):
            taus = [t for t in range(taps) if (off + t * shift) % 8 == phase]
            if not taus:
                continue
            span = max(off + t * shift - phase for t in taus) + tm
            win_ref[0:span, :] = up_ref[phase:phase + span, :]
            for t in taus:
                q = off + t * shift - phase
                dw = dw + win_ref[q:q + tm, :] * cw_ref[t:t + 1, :]
    nb_ref[...] = up_ref[tm:tm + hist, :]
    if tiles_per_seq > 1:
        up_ref[0:hist, :] = up_ref[tm:tm + hist, :]

    mu = jnp.mean(dw, axis=-1, keepdims=True)
    dd = dw - mu
    var = jnp.mean(dd * dd, axis=-1, keepdims=True)
    ln = dd * lax.rsqrt(var + LN_EPS) * clg_ref[...] + clb_ref[...]
    ob = _dot(ln * _sigmoid(ln), wco_ref[...])
    ga = _sigmoid(jnp.dot(hb, wb_ref[:, 2 * dc:2 * dc + d], preferred_element_type=F32))
    gb = _sigmoid(jnp.dot(hb, wb_ref[:, 2 * dc + d:], preferred_element_type=F32))
    z = ga * oa_ref[...] + gb * ob
    o_ref[...] = x + _dot(z, wo_ref[...])


def _mix_out(x, oa, hist, p, layer, *, shift, seq_rows):
    m, d = x.shape
    dc = p['wco'].shape[1]
    taps = p['conv_w'].shape[1]
    wbw = p['wb'].shape[-1]
    hrows = hist.shape[1]
    if shift == 1:
        tm = _pick_tile(seq_rows, 256)
        tps = seq_rows // tm
    else:
        tm = m
        tps = 1
    nt = m // tm
    assert tps == 1 or tm >= hrows
    tile = lambda: pl.BlockSpec((tm, d), lambda i: (i, 0))
    body = functools.partial(_mixout_body, shift=shift, tiles_per_seq=tps, d=d, dc=dc, taps=taps)
    return pl.pallas_call(
        body,
        grid=(nt,),
        in_specs=[
            tile(), tile(),
            pl.BlockSpec((None, hrows, dc), lambda i: (i // tps, 0, 0)),
            pl.BlockSpec((None, 1, d), lambda i: (layer, 0, 0)),
            pl.BlockSpec((None, d, wbw), lambda i: (layer, 0, 0)),
            pl.BlockSpec((None, taps, dc), lambda i: (layer, 0, 0)),
            pl.BlockSpec((None, 1, dc), lambda i: (layer, 0, 0)),
            pl.BlockSpec((None, 1, dc), lambda i: (layer, 0, 0)),
            pl.BlockSpec((None, 1, dc), lambda i: (layer, 0, 0)),
            pl.BlockSpec((None, dc, d), lambda i: (layer, 0, 0)),
            pl.BlockSpec((None, d, d), lambda i: (layer, 0, 0)),
        ],
        out_specs=[tile(), pl.BlockSpec((None, hrows, dc), lambda i: (i // tps, 0, 0))],
        out_shape=[jax.ShapeDtypeStruct((m, d), F32),
                   jax.ShapeDtypeStruct((nt // tps, hrows, dc), F32)],
        scratch_shapes=[pltpu.VMEM((hrows + tm, dc), F32),
                        pltpu.VMEM((hrows + tm if shift % 8 else 8, dc), F32)],
        compiler_params=_cparams(("arbitrary",)),
        name="mix_out",
    )(x, oa, hist, p['mix_norm'], p['wb'], p['conv_w'], p['conv_b'], p['conv_ln_g'],
      p['conv_ln_b'], p['wco'], p['wo'])


def _norm_body(x_ref, g_ref, o_ref):
    o_ref[...] = _rmsnorm(x_ref[...], g_ref[...])


def _final_norm(x, g):
    m, d = x.shape
    tm = _pick_tile(m, 1024)
    return pl.pallas_call(
        _norm_body,
        grid=(m // tm,),
        in_specs=[pl.BlockSpec((tm, d), lambda i: (i, 0)), pl.BlockSpec((1, d), lambda i: (0, 0))],
        out_specs=pl.BlockSpec((tm, d), lambda i: (i, 0)),
        out_shape=jax.ShapeDtypeStruct((m, d), F32),
        compiler_params=_cparams(("parallel",)),
        name="final_norm",
    )(x, g)


def kernel(x_prompt, x_sample, state_wkv, state_shift, state_conv, ffn1_norm, ffn1_w_gate, ffn1_w_up, ffn1_w_down, mix_norm, w_in, mu, w0, w_decay_up, a0, w_iclr_up, w_gate_up, k_k, k_a, r_k, ln_x_g, ln_x_b, conv_w, conv_b, conv_ln_g, conv_ln_b, w_conv_out, w_out, ffn2_norm, ffn2_w_gate, ffn2_w_up, ffn2_w_down, final_norm):
    bp, tp, d = x_prompt.shape
    bs, ts, _ = x_sample.shape
    depth = w_in.shape[0]
    heads = d // HEAD
    dc = conv_w.shape[-1]
    taps = conv_w.shape[1]
    ld, li, lg = w_decay_up.shape[1], w_iclr_up.shape[1], w_gate_up.shape[1]
    sw = 3 * d + ld + li + lg
    row = lambda t: t[:, None, :]
    seg = jnp.arange(SEG) // HEAD
    p = {
        'mix_norm': row(mix_norm), 'mu': row(mu), 'w0': row(w0), 'a0': row(a0), 'k_k': row(k_k),
        'k_a': row(k_a), 'r_k': r_k.reshape(depth, 1, d), 'ln_x_g': row(ln_x_g), 'ln_x_b': row(ln_x_b),
        'conv_w': conv_w, 'conv_b': row(conv_b), 'conv_ln_g': row(conv_ln_g), 'conv_ln_b': row(conv_ln_b),
        'wa': w_in[:, :, :sw].astype(BF16), 'wb': w_in[:, :, sw:].astype(BF16),
        'wdec': w_decay_up.astype(BF16), 'wiclr': w_iclr_up.astype(BF16), 'wgate': w_gate_up.astype(BF16),
        'wco': w_conv_out.astype(BF16), 'wo': w_out.astype(BF16),
        'bo': (seg[:, None] == seg[None, :]).astype(BF16),
    }
    f1 = (row(ffn1_norm), ffn1_w_gate.astype(BF16), ffn1_w_up.astype(BF16), ffn1_w_down.astype(BF16))
    f2 = (row(ffn2_norm), ffn2_w_gate.astype(BF16), ffn2_w_up.astype(BF16), ffn2_w_down.astype(BF16))

    xp = x_prompt.reshape(bp * tp, d)
    xs = jnp.swapaxes(x_sample, 0, 1).reshape(ts * bs, d)
    hist_rows_p = -(-(taps - 1) // 8) * 8
    zeros_sp = jnp.zeros((8, d), F32)
    zeros_hist = jnp.zeros((bp, hist_rows_p, dc), F32)
    zeros_s0 = jnp.zeros((bp, heads, HEAD, HEAD), F32)
    chunk_p = _pick_tile(tp, 64)
    ts_pad = -(-ts // 8) * 8

    def to_bm(t):
        t = jnp.swapaxes(t.reshape(ts, bs, d), 0, 1)
        return jnp.pad(t, ((0, 0), (0, ts_pad - ts), (0, 0)))

    wkv_p, shift_p, conv_p, wkv_s, shift_s, conv_s = [], [], [], [], [], []
    for l in range(depth):
        xp = _ffn(xp, *f1, l)
        *seqs, hs = _mix_in(xp, zeros_sp, p, l, shift=1, seq_rows=tp)
        seqs = [t.reshape(bp, tp, d) for t in seqs]
        oa, s_new = _wkv(*seqs, zeros_s0, p, l, chunk=chunk_p, gb=4 if bp % 4 == 0 else 1)
        xp, nbuf = _mix_out(xp, oa.reshape(bp * tp, d), zeros_hist, p, l, shift=1, seq_rows=tp)
        xp = _ffn(xp, *f2, l)
        wkv_p.append(s_new)
        shift_p.append(hs[:, -1, :])
        conv_p.append(nbuf[:, hist_rows_p - (taps - 1):, :])
        xs = _ffn(xs, *f1, l)
        *seqs, hs = _mix_in(xs, state_shift[l], p, l, shift=bs, seq_rows=ts * bs)
        seqs = [to_bm(t) for t in seqs]
        oa, s_new = _wkv(*seqs, state_wkv[l], p, l, chunk=ts_pad, gb=2 if bs % 2 == 0 else 1)
        oa = jnp.swapaxes(oa[:, :ts, :], 0, 1).reshape(ts * bs, d)
        hist = jnp.swapaxes(state_conv[l], 0, 1).reshape(1, (taps - 1) * bs, dc)
        xs, nbuf = _mix_out(xs, oa, hist, p, l, shift=bs, seq_rows=ts * bs)
        xs = _ffn(xs, *f2, l)
        wkv_s.append(s_new)
        shift_s.append(hs[0])
        conv_s.append(jnp.swapaxes(nbuf.reshape(taps - 1, bs, dc), 0, 1))

    gfin = final_norm[None, :]
    y_prompt = _final_norm(xp, gfin).reshape(bp, tp, d)
    y_sample = jnp.swapaxes(_final_norm(xs, gfin).reshape(ts, bs, d), 0, 1)
    return (y_prompt, y_sample, jnp.stack(wkv_p), jnp.stack(shift_p), jnp.stack(conv_p),
            jnp.stack(wkv_s), jnp.stack(shift_s), jnp.stack(conv_s))
```
